```python
import jax, jax.numpy as jnp
from jax import lax
import numpy as np

D_MODEL = 1024
BATCH = 2
SEQ = 8192
DEPTH = 2

CHUNK = 64
POOL_WIDTH = D_MODEL // 2
POOL_WINDOWS = (2, 4, 8, 16)
POOL_GROUPS = len(POOL_WINDOWS)
POOL_GROUP_DIM = POOL_WIDTH // POOL_GROUPS
HGRN_HEAD_DIM = 128
HGRN_HEADS = (D_MODEL // 2) // HGRN_HEAD_DIM
HGRN_WIDTH = HGRN_HEADS * HGRN_HEAD_DIM
N_BRANCHES = 2
IN_COLS = POOL_WIDTH + 4 * HGRN_WIDTH + N_BRANCHES * D_MODEL
D_FF = 128 * ((8 * D_MODEL // 3 + 127) // 128)
CONV_WIDTH = 3
EPS = 1e-6

kernel_name = "hybrid_pool_hgrn2_convglu_trunk"


def rmsnorm(x, g):
    xf = x.astype(jnp.float32)
    y = xf * lax.rsqrt(jnp.mean(xf * xf, axis=-1, keepdims=True) + EPS) * g.astype(jnp.float32)
    return y.astype(x.dtype)


def pool_mixer(u, pool_w, pool_scale):
    B, S, _ = u.shape
    ug = u.reshape(B, S, POOL_GROUPS, POOL_GROUP_DIM).astype(jnp.float32)
    cs = jnp.cumsum(ug, axis=1)
    t = jnp.arange(1, S + 1, dtype=jnp.float32)
    outs = []
    for g, w in enumerate(POOL_WINDOWS):
        c_pad = jnp.pad(cs[:, :, g], ((0, 0), (w, 0), (0, 0)))
        win_sum = c_pad[:, w:] - c_pad[:, :S]
        count = jnp.minimum(t, float(w))[None, :, None]
        outs.append(win_sum / count - ug[:, :, g])
    pooled = jnp.stack(outs, axis=2).astype(u.dtype)
    mixed = jnp.einsum('bsgc,gcd->bsgd', pooled, pool_w).reshape(B, S, POOL_WIDTH)
    return mixed * pool_scale


def hgrn2_mixer(zq, zf, zi, zo, lb, norm_g):
    B, S, _ = zq.shape
    n_chunks = S // CHUNK
    H, Dh = HGRN_HEADS, HGRN_HEAD_DIM

    def heads(a):
        return a.reshape(B, S, H, Dh).astype(jnp.float32)

    q = jax.nn.silu(heads(zq))
    lbh = lb.reshape(H, Dh).astype(jnp.float32)
    f = lbh + (1.0 - lbh) * jax.nn.sigmoid(heads(zf))
    log_f = jnp.log(f)
    k = 1.0 - f
    v = heads(zi)

    def to_chunks(a):
        return a.reshape(B, n_chunks, CHUNK, H, Dh).transpose(1, 0, 3, 2, 4)

    causal = jnp.tril(jnp.ones((CHUNK, CHUNK), dtype=bool))

    def step(state, xs):
        qc, kc, vc, gc = xs
        b = jnp.cumsum(gc, axis=2)
        diff = b[:, :, :, None, :] - b[:, :, None, :, :]
        decay = jnp.exp(jnp.where(causal[:, :, None], diff, -jnp.inf))
        attn = jnp.einsum('bhtk,bhtsk,bhsk->bhts', qc, decay, kc)
        o = (jnp.einsum('bhts,bhsv->bhtv', attn, vc)
             + jnp.einsum('bhtk,bhkv->bhtv', qc * jnp.exp(b), state))
        b_last = b[:, :, -1:, :]
        new_state = (jnp.exp(b_last[:, :, 0, :])[..., None] * state
                     + jnp.einsum('bhsk,bhsv->bhkv', kc * jnp.exp(b_last - b), vc))
        return new_state, o

    s0 = jnp.zeros((B, H, Dh, Dh), jnp.float32)
    _, o = lax.scan(step, s0, (to_chunks(q), to_chunks(k), to_chunks(v), to_chunks(log_f)))
    o = o.transpose(1, 0, 3, 2, 4).reshape(B, S, H, Dh)
    o = o * lax.rsqrt(jnp.mean(o * o, axis=-1, keepdims=True) + EPS) * norm_g.astype(jnp.float32)
    o = o * jax.nn.silu(heads(zo))
    return o.reshape(B, S, HGRN_WIDTH).astype(zq.dtype)


def conv_glu_ffn(x, w_up, conv_w, conv_b, w_down):
    S = x.shape[1]
    h = x @ w_up
    hp = jnp.pad(h, ((0, 0), (CONV_WIDTH - 1, 0), (0, 0)))
    hc = sum(conv_w[j] * hp[:, j:j + S] for j in range(CONV_WIDTH)) + conv_b
    val, gate = jnp.split(hc, 2, axis=-1)
    return (jax.nn.silu(gate) * val) @ w_down


def setup_inputs(seed: int = 0) -> dict:
    key = jax.random.key(seed)
    ks = jax.random.split(key, 20)
    f32 = jnp.float32
    nrm = lambda k, shape, scale: (jax.random.normal(k, shape, f32) * scale).astype(f32)
    L = DEPTH
    return {
        "x": nrm(ks[0], (BATCH, SEQ, D_MODEL), 1.0),
        "norm1_g": 1.0 + nrm(ks[1], (L, D_MODEL), 0.02),
        "w_in": nrm(ks[2], (L, D_MODEL, IN_COLS), D_MODEL ** -0.5),
        "b_gate": nrm(ks[3], (L, N_BRANCHES * D_MODEL), 0.01),
        "pool_w": nrm(ks[4], (L, POOL_GROUPS, POOL_GROUP_DIM, POOL_GROUP_DIM), POOL_GROUP_DIM ** -0.5),
        "pool_scale": 1.0 + nrm(ks[5], (L, POOL_WIDTH), 0.02),
        "lb_logits": nrm(ks[6], (L, HGRN_WIDTH), 1.0),
        "hgrn_norm_g": 1.0 + nrm(ks[7], (L, HGRN_HEAD_DIM), 0.02),
        "w_pa": nrm(ks[8], (L, POOL_WIDTH, D_MODEL), POOL_WIDTH ** -0.5),
        "w_pb": nrm(ks[9], (L, HGRN_WIDTH, D_MODEL), HGRN_WIDTH ** -0.5),
        "w_o": nrm(ks[10], (L, D_MODEL, D_MODEL), D_MODEL ** -0.5),
        "norm2_g": 1.0 + nrm(ks[11], (L, D_MODEL), 0.02),
        "w_up": nrm(ks[12], (L, D_MODEL, 2 * D_FF), D_MODEL ** -0.5),
        "conv_w": nrm(ks[13], (L, CONV_WIDTH, 2 * D_FF), CONV_WIDTH ** -0.5),
        "conv_b": nrm(ks[14], (L, 2 * D_FF), 0.01),
        "w_down": nrm(ks[15], (L, D_FF, D_MODEL), D_FF ** -0.5),
        "final_g": 1.0 + nrm(ks[16], (D_MODEL,), 0.02),
    }


def reference(x, norm1_g, w_in, b_gate, pool_w, pool_scale, lb_logits, hgrn_norm_g,
              w_pa, w_pb, w_o, norm2_g, w_up, conv_w, conv_b, w_down, final_g):
    B, S, _ = x.shape
    lb_soft = jax.nn.softmax(lb_logits.astype(jnp.float32), axis=0)
    lb_cum = jnp.cumsum(lb_soft, axis=0)
    lower_bounds = lb_cum - lb_cum[0:1]

    splits = np.cumsum([POOL_WIDTH, HGRN_WIDTH, HGRN_WIDTH, HGRN_WIDTH, HGRN_WIDTH]).tolist()
    for l in range(DEPTH):
        xn = rmsnorm(x, norm1_g[l])
        z = xn @ w_in[l]
        u_pool, zq, zf, zi, zo, zg = jnp.split(z, splits, axis=-1)
        gates = jax.nn.sigmoid(zg + b_gate[l]).reshape(B, S, N_BRANCHES, D_MODEL)
        ya = pool_mixer(u_pool, pool_w[l], pool_scale[l]) @ w_pa[l]
        yb = hgrn2_mixer(zq, zf, zi, zo, lower_bounds[l], hgrn_norm_g[l]) @ w_pb[l]
        merged = gates[:, :, 0] * ya + gates[:, :, 1] * yb
        x = x + merged @ w_o[l]
        x = x + conv_glu_ffn(rmsnorm(x, norm2_g[l]), w_up[l], conv_w[l], conv_b[l], w_down[l])
    return rmsnorm(x, final_g)
```

```python
import functools

import numpy as np
import jax
import jax.numpy as jnp
from jax import lax
from jax.experimental import pallas as pl
from jax.experimental.pallas import tpu as pltpu

EPS = 1e-6
POOL_WINDOWS = (2, 4, 8, 16)
POOL_HISTORY = 16
HEAD_DIM = 128
HGRN_CHUNK = 128
CONV_WIDTH = 3
CONV_TAIL = 8
V7X_VMEM_LIMIT_BYTES = 56 * 1024 * 1024

_F32 = jnp.float32
_BF16 = jnp.bfloat16


def _sigmoid(x):
    return 0.5 * jnp.tanh(0.5 * x) + 0.5


def _silu(x):
    return x * _sigmoid(x)


def _rmsnorm(x, g):
    return x * lax.rsqrt(jnp.mean(x * x, axis=-1, keepdims=True) + EPS) * g


def _dot(a, b):
    return jnp.dot(a, b, preferred_element_type=_F32)


def _dot_nt(a, b):
    return lax.dot_general(a, b, (((1,), (1,)), ((), ())), preferred_element_type=_F32)


def _dot_tn(a, b):
    return lax.dot_general(a, b, (((0,), (0,)), ((), ())), preferred_element_type=_F32)


def _hierarchy_tables(chunk):
    n_levels = int(np.log2(chunk))
    assert 1 << n_levels == chunk
    t = np.arange(chunk)[:, None]
    u = np.arange(chunk)[None, :]
    mats = []
    for l in range(n_levels):
        m = 1 << l
        p = t % (2 * m)
        upper = p >= m
        block_start = t - p + m
        block_end = t - p + m - 1
        mats.append(np.where(upper, (u >= block_start) & (u <= t), (u > t) & (u <= block_end)))
    mats.append(u <= t)
    mats.append(u > t)
    dmat = np.concatenate(mats, axis=0).astype(np.float32)
    xor = t ^ u
    level = np.full((chunk, chunk), n_levels + 8, np.int32)
    lower = t > u
    level[lower] = np.floor(np.log2(xor[lower])).astype(np.int32)
    level[np.arange(chunk), np.arange(chunk)] = -1
    return dmat, level, n_levels


def _mixer_kernel(x_ref, n1g_ref, w_in_ref, bg_ref, pw_ref, ps_ref, lbl_ref, hng_ref, w_pa_ref, w_pb_ref,
                  w_o_ref, dmat_ref, lvl_ref, out_ref, z_scr, o_scr, uh_scr, st_scr, *, layer, n_levels):
    tm, d_model = x_ref.shape
    n_heads = st_scr.shape[0]
    width = n_heads * HEAD_DIM
    chunk = HGRN_CHUNK
    i = pl.program_id(1)

    @pl.when(i == 0)
    def _():
        uh_scr[...] = jnp.zeros_like(uh_scr)
        st_scr[...] = jnp.zeros_like(st_scr)

    x = x_ref[...]
    xn = _rmsnorm(x, n1g_ref[...]).astype(_BF16)

    in_cols = w_in_ref.shape[1]
    for j in range(in_cols // width):
        z_scr[:, j * width:(j + 1) * width] = _dot(xn, w_in_ref[:, j * width:(j + 1) * width])

    u = z_scr[:, 0:width]
    ext = jnp.concatenate([uh_scr[...], u], axis=0)
    uh_scr[...] = u[tm - POOL_HISTORY:, :]
    sums = ext
    frames_seen = (i * tm + 1 + lax.broadcasted_iota(jnp.int32, (tm, HEAD_DIM), 0)).astype(_F32)
    mixed = []
    for gi, w in enumerate(POOL_WINDOWS):
        sums = sums[:, (HEAD_DIM if gi else 0):]
        sums = sums + pltpu.roll(sums, w // 2, 0)
        win = sums[POOL_HISTORY:, 0:HEAD_DIM]
        u_g = u[:, gi * HEAD_DIM:(gi + 1) * HEAD_DIM]
        pooled = win / jnp.minimum(frames_seen, float(w)) - u_g
        mixed.append(_dot(pooled.astype(_BF16), pw_ref[gi]))
    mixed = jnp.concatenate(mixed, axis=1) * ps_ref[...]
    ya = _dot(mixed.astype(_BF16), w_pa_ref[...])

    logits = lbl_ref[...]
    e = jnp.exp(logits - jnp.max(logits, axis=0, keepdims=True))
    soft = e / jnp.sum(e, axis=0, keepdims=True)
    lb = jnp.zeros((1, width), _F32)
    for r in range(1, layer + 1):
        lb = lb + soft[r:r + 1, :]

    level = lvl_ref[...]
    hng = hng_ref[...]

    def chunk_body(c, carry):
        r0 = pl.multiple_of(c * chunk, chunk)
        rows = pl.ds(r0, chunk)
        q = _silu(z_scr[rows, width:2 * width])
        f = lb + (1.0 - lb) / (1.0 + jnp.exp(-z_scr[rows, 2 * width:3 * width]))
        g = jnp.log(f)
        kk = 1.0 - f
        v = z_scr[rows, 3 * width:4 * width].astype(_BF16)
        g_hi = g.astype(_BF16)
        g_lo = (g - g_hi.astype(_F32)).astype(_BF16)

        def exponent(idx):
            d = dmat_ref[idx * chunk:(idx + 1) * chunk, :]
            return _dot(d, g_hi) + _dot(d, g_lo)

        def head(a, h):
            return a[:, h * HEAD_DIM:(h + 1) * HEAD_DIM]

        q_b = q.astype(_BF16)
        k_b = kk.astype(_BF16)
        scores = [jnp.where(level == -1, _dot_nt(head(q_b, h), head(k_b, h)), 0.0) for h in range(n_heads)]
        for l in range(n_levels):
            wgt = jnp.exp(exponent(l))
            q_l = (q * wgt).astype(_BF16)
            k_l = (kk * wgt).astype(_BF16)
            for h in range(n_heads):
                scores[h] = jnp.where(level == l, _dot_nt(head(q_l, h), head(k_l, h)), scores[h])

        b = exponent(n_levels)
        b_rest = exponent(n_levels + 1)
        q_in = (q * jnp.exp(b)).astype(_BF16)
        k_out = (kk * jnp.exp(b_rest)).astype(_BF16)
        chunk_decay = jnp.exp(b[chunk - 1:chunk, :])
        zo = z_scr[rows, 4 * width:5 * width]
        for h in range(n_heads):
            st = st_scr[h]
            o = _dot(scores[h].astype(_BF16), head(v, h)) + _dot_nt(head(q_in, h), st.astype(_BF16))
            st_scr[h] = head(chunk_decay, h) * st + _dot_tn(head(v, h), head(k_out, h))
            o = o * lax.rsqrt(jnp.mean(o * o, axis=-1, keepdims=True) + EPS) * hng
            o_scr[rows, h * HEAD_DIM:(h + 1) * HEAD_DIM] = o * _silu(head(zo, h))
        return carry

    lax.fori_loop(0, tm // chunk, chunk_body, 0)
    yb = _dot(o_scr[...].astype(_BF16), w_pb_ref[...])

    gates = _sigmoid(z_scr[:, 5 * width:] + bg_ref[...])
    merged = gates[:, :d_model] * ya + gates[:, d_model:] * yb
    out_ref[...] = x + _dot(merged.astype(_BF16), w_o_ref[...])


def _ffn_kernel(x_ref, n2g_ref, w_up_ref, cw_ref, cb_ref, w_dn_ref, fg_ref, out_ref, a_scr, tail_scr, *,
                block, final_norm):
    tm, d_model = x_ref.shape
    d_ff = w_dn_ref.shape[0]
    i = pl.program_id(1)

    @pl.when(i == 0)
    def _():
        tail_scr[...] = jnp.zeros_like(tail_scr)

    x = x_ref[...]
    xn = _rmsnorm(x, n2g_ref[...]).astype(_BF16)
    row = lax.broadcasted_iota(jnp.int32, (tm, block), 0)

    def conv_block(c0):
        h = _dot(xn, w_up_ref[:, c0:c0 + block])
        tail = tail_scr[:, c0:c0 + block]
        tail_scr[:, c0:c0 + block] = h[tm - CONV_TAIL:, :]
        prev1 = tail[CONV_TAIL - 1:CONV_TAIL, :]
        prev2 = tail[CONV_TAIL - 2:CONV_TAIL - 1, :]
        h1 = jnp.where(row == 0, prev1, pltpu.roll(h, 1, 0))
        h2 = jnp.where(row == 0, prev2, jnp.where(row == 1, prev1, pltpu.roll(h, 2, 0)))
        cw = cw_ref[:, c0:c0 + block]
        return cw[2:3, :] * h + cw[1:2, :] * h1 + cw[0:1, :] * h2 + cb_ref[:, c0:c0 + block]

    for j in range(d_ff // block):
        val = conv_block(j * block)
        gate = conv_block(d_ff + j * block)
        a_scr[:, j * block:(j + 1) * block] = (_silu(gate) * val).astype(_BF16)
    y = x + _dot(a_scr[...], w_dn_ref[...])
    if final_norm:
        y = _rmsnorm(y, fg_ref[...])
    out_ref[...] = y


def _resident(shape):
    zeros = (0,) * len(shape)
    return pl.BlockSpec(shape, lambda b, i: zeros, pipeline_mode=pl.Buffered(1))


def _token_tile(seq):
    for tm in (512, 256, 128):
        if seq % tm == 0:
            return tm
    raise ValueError(f"sequence length {seq} must be a multiple of {HGRN_CHUNK}")


def _params():
    return pltpu.CompilerParams(dimension_semantics=("arbitrary", "arbitrary"),
                                vmem_limit_bytes=V7X_VMEM_LIMIT_BYTES)


def _mixer(x, n1g, w_in, bg, pw, ps, lbl, hng, w_pa, w_pb, w_o, *, layer):
    batch, seq, d_model = x.shape
    tm = _token_tile(seq)
    width = w_pa.shape[0]
    n_heads = width // HEAD_DIM
    dmat, level, n_levels = _hierarchy_tables(HGRN_CHUNK)
    consts = (n1g, w_in, bg, pw, ps, lbl, hng, w_pa, w_pb, w_o, jnp.asarray(dmat, _BF16), jnp.asarray(level))
    tile = pl.BlockSpec((None, tm, d_model), lambda b, i: (b, i, 0))
    return pl.pallas_call(
        functools.partial(_mixer_kernel, layer=layer, n_levels=n_levels),
        out_shape=jax.ShapeDtypeStruct(x.shape, _F32),
        grid=(batch, seq // tm),
        in_specs=[tile] + [_resident(c.shape) for c in consts],
        out_specs=tile,
        scratch_shapes=[
            pltpu.VMEM((tm, w_in.shape[1]), _F32),
            pltpu.VMEM((tm, width), _F32),
            pltpu.VMEM((POOL_HISTORY, width), _F32),
            pltpu.VMEM((n_heads, HEAD_DIM, HEAD_DIM), _F32),
        ],
        compiler_params=_params(),
        name=f"mixer_l{layer}",
    )(x, *consts)


def _ffn(x, n2g, w_up, cw, cb, w_dn, fg, *, layer, final_norm):
    batch, seq, d_model = x.shape
    tm = _token_tile(seq)
    d_ff = w_dn.shape[0]
    block = 256 if d_ff % 256 == 0 else 128
    consts = (n2g, w_up, cw, cb, w_dn, fg)
    tile = pl.BlockSpec((None, tm, d_model), lambda b, i: (b, i, 0))
    return pl.pallas_call(
        functools.partial(_ffn_kernel, block=block, final_norm=final_norm),
        out_shape=jax.ShapeDtypeStruct(x.shape, _F32),
        grid=(batch, seq // tm),
        in_specs=[tile] + [_resident(c.shape) for c in consts],
        out_specs=tile,
        scratch_shapes=[
            pltpu.VMEM((tm, d_ff), _BF16),
            pltpu.VMEM((CONV_TAIL, 2 * d_ff), _F32),
        ],
        compiler_params=_params(),
        name=f"ffn_l{layer}",
    )(x, *consts)


def kernel(x, norm1_g, w_in, b_gate, pool_w, pool_scale, lb_logits, hgrn_norm_g, w_pa, w_pb, w_o, norm2_g,
           w_up, conv_w, conv_b, w_down, final_g):
    depth = w_in.shape[0]
    x = x.astype(_F32)
    lbl = lb_logits.astype(_F32)
    for l in range(depth):
        x = _mixer(x, norm1_g[l][None, :], w_in[l].astype(_BF16), b_gate[l][None, :], pool_w[l].astype(_BF16),
                   pool_scale[l][None, :], lbl, hgrn_norm_g[l][None, :], w_pa[l].astype(_BF16),
                   w_pb[l].astype(_BF16), w_o[l].astype(_BF16), layer=l)
        x = _ffn(x, norm2_g[l][None, :], w_up[l].astype(_BF16), conv_w[l], conv_b[l][None, :],
                 w_down[l].astype(_BF16), final_g[None, :], layer=l, final_norm=(l == depth - 1))
    return x
```

```python
import functools

import numpy as np
import jax
import jax.numpy as jnp
from jax import lax
from jax.experimental import pallas as pl
from jax.experimental.pallas import tpu as pltpu

EPS = 1e-6
POOL_WINDOWS = (2, 4, 8, 16)
POOL_HISTORY = 16
HEAD_DIM = 128
HGRN_CHUNK = 128
CONV_WIDTH = 3
CONV_TAIL = 8
V7X_VMEM_LIMIT_BYTES = 56 * 1024 * 1024

_F32 = jnp.float32
_BF16 = jnp.bfloat16


def _sigmoid(x):
    return 0.5 * jnp.tanh(0.5 * x) + 0.5


def _silu(x):
    return x * _sigmoid(x)


def _rmsnorm(x, g):
    return x * lax.rsqrt(jnp.mean(x * x, axis=-1, keepdims=True) + EPS) * g


def _dot(a, b):
    return jnp.dot(a, b, preferred_element_type=_F32)


def _dot_nt(a, b):
    return lax.dot_general(a, b, (((1,), (1,)), ((), ())), preferred_element_type=_F32)


def _dot_tn(a, b):
    return lax.dot_general(a, b, (((0,), (0,)), ((), ())), preferred_element_type=_F32)


def _hierarchy_tables(chunk):
    n_levels = int(np.log2(chunk))
    assert 1 << n_levels == chunk
    t = np.arange(chunk)[:, None]
    u = np.arange(chunk)[None, :]
    mats = []
    for l in range(n_levels):
        m = 1 << l
        p = t % (2 * m)
        upper = p >= m
        block_start = t - p + m
        block_end = t - p + m - 1
        mats.append(np.where(upper, (u >= block_start) & (u <= t), (u > t) & (u <= block_end)))
    mats.append(u <= t)
    mats.append(u > t)
    dmat = np.concatenate(mats, axis=0).astype(np.float32)
    xor = t ^ u
    level = np.full((chunk, chunk), n_levels + 8, np.int32)
    lower = t > u
    level[lower] = np.floor(np.log2(xor[lower])).astype(np.int32)
    level[np.arange(chunk), np.arange(chunk)] = -1
    return dmat, level, n_levels


def _mixer_kernel(x_ref, n1g_ref, w_in_ref, bg_ref, pw_ref, ps_ref, lbl_ref, hng_ref, w_pa_ref, w_pb_ref,
                  w_o_ref, dmat_ref, lvl_ref, out_ref, z_scr, o_scr, uh_scr, st_scr, *, layer, n_levels):
    tm, d_model = x_ref.shape
    n_heads = st_scr.shape[0]
    width = n_heads * HEAD_DIM
    chunk = HGRN_CHUNK
    i = pl.program_id(1)

    @pl.when(i == 0)
    def _():
        uh_scr[...] = jnp.zeros_like(uh_scr)
        st_scr[...] = jnp.zeros_like(st_scr)

    x = x_ref[...]
    xn = _rmsnorm(x, n1g_ref[...]).astype(_BF16)

    in_cols = w_in_ref.shape[1]
    for j in range(in_cols // width):
        z_scr[:, j * width:(j + 1) * width] = _dot(xn, w_in_ref[:, j * width:(j + 1) * width])

    u = z_scr[:, 0:width]
    ext = jnp.concatenate([uh_scr[...], u], axis=0)
    uh_scr[...] = u[tm - POOL_HISTORY:, :]
    sums = ext
    frames_seen = (i * tm + 1 + lax.broadcasted_iota(jnp.int32, (tm, HEAD_DIM), 0)).astype(_F32)
    mixed = []
    for gi, w in enumerate(POOL_WINDOWS):
        sums = sums[:, (HEAD_DIM if gi else 0):]
        sums = sums + pltpu.roll(sums, w // 2, 0)
        win = sums[POOL_HISTORY:, 0:HEAD_DIM]
        u_g = u[:, gi * HEAD_DIM:(gi + 1) * HEAD_DIM]
        pooled = win / jnp.minimum(frames_seen, float(w)) - u_g
        mixed.append(_dot(pooled.astype(_BF16), pw_ref[gi]))
    mixed = jnp.concatenate(mixed, axis=1) * ps_ref[...]
    ya = _dot(mixed.astype(_BF16), w_pa_ref[...])

    logits = lbl_ref[...]
    e = jnp.exp(logits - jnp.max(logits, axis=0, keepdims=True))
    soft = e / jnp.sum(e, axis=0, keepdims=True)
    lb = jnp.zeros((1, width), _F32)
    for r in range(1, layer + 1):
        lb = lb + soft[r:r + 1, :]

    level = lvl_ref[...]
    hng = hng_ref[...]

    def chunk_body(c, carry):
        rows = pl.ds(c * chunk, chunk)
        q = _silu(z_scr[rows, width:2 * width])
        f = lb + (1.0 - lb) / (1.0 + jnp.exp(-z_scr[rows, 2 * width:3 * width]))
        g = jnp.log(f)
        kk = 1.0 - f
        v = z_scr[rows, 3 * width:4 * width].astype(_BF16)
        g_hi = g.astype(_BF16)
        g_lo = (g - g_hi.astype(_F32)).astype(_BF16)
        g_split = jnp.concatenate([g_hi, g_lo], axis=0)

        def exponent(idx):
            return _dot(dmat_ref[idx * chunk:(idx + 1) * chunk, :], g_split)

        def head(a, h):
            return a[:, h * HEAD_DIM:(h + 1) * HEAD_DIM]

        q_b = q.astype(_BF16)
        k_b = kk.astype(_BF16)
        scores = [jnp.where(level == -1, _dot_nt(head(q_b, h), head(k_b, h)), 0.0) for h in range(n_heads)]
        for l in range(n_levels):
            if l == 0:
                q_l, k_l = (q * f).astype(_BF16), k_b
            else:
                wgt = jnp.exp(exponent(l)).astype(_BF16)
                q_l, k_l = q_b * wgt, k_b * wgt
            for h in range(n_heads):
                scores[h] = jnp.where(level == l, _dot_nt(head(q_l, h), head(k_l, h)), scores[h])

        b = exponent(n_levels)
        b_rest = exponent(n_levels + 1)
        q_in = q_b * jnp.exp(b).astype(_BF16)
        k_out = k_b * jnp.exp(b_rest).astype(_BF16)
        chunk_decay = jnp.exp(b[chunk - 1:chunk, :])
        zo = z_scr[rows, 4 * width:5 * width]
        for h in range(n_heads):
            st = st_scr[h]
            o = _dot(scores[h].astype(_BF16), head(v, h)) + _dot_nt(head(q_in, h), st.astype(_BF16))
            st_scr[h] = head(chunk_decay, h) * st + _dot_tn(head(v, h), head(k_out, h))
            o = o * lax.rsqrt(jnp.mean(o * o, axis=-1, keepdims=True) + EPS) * hng
            o_scr[rows, h * HEAD_DIM:(h + 1) * HEAD_DIM] = o * _silu(head(zo, h))
        return carry

    for c in range(tm // chunk):
        chunk_body(c, 0)
    yb = _dot(o_scr[...].astype(_BF16), w_pb_ref[...])

    gates = _sigmoid(z_scr[:, 5 * width:] + bg_ref[...])
    merged = gates[:, :d_model] * ya + gates[:, d_model:] * yb
    out_ref[...] = x + _dot(merged.astype(_BF16), w_o_ref[...])


def _ffn_kernel(x_ref, n2g_ref, w_up_ref, cw_ref, cb_ref, w_dn_ref, fg_ref, out_ref, a_scr, tail_scr, *,
                block, final_norm):
    tm, d_model = x_ref.shape
    d_ff = w_dn_ref.shape[0]
    i = pl.program_id(1)

    @pl.when(i == 0)
    def _():
        tail_scr[...] = jnp.zeros_like(tail_scr)

    x = x_ref[...]
    xn = _rmsnorm(x, n2g_ref[...]).astype(_BF16)
    row = lax.broadcasted_iota(jnp.int32, (tm, block), 0)

    def conv_block(c0):
        h = _dot(xn, w_up_ref[:, c0:c0 + block])
        tail = tail_scr[:, c0:c0 + block]
        tail_scr[:, c0:c0 + block] = h[tm - CONV_TAIL:, :]
        prev1 = tail[CONV_TAIL - 1:CONV_TAIL, :]
        prev2 = tail[CONV_TAIL - 2:CONV_TAIL - 1, :]
        h1 = jnp.where(row == 0, prev1, pltpu.roll(h, 1, 0))
        h2 = jnp.where(row == 0, prev2, jnp.where(row == 1, prev1, pltpu.roll(h, 2, 0)))
        cw = cw_ref[:, c0:c0 + block]
        return cw[2:3, :] * h + cw[1:2, :] * h1 + cw[0:1, :] * h2 + cb_ref[:, c0:c0 + block]

    for j in range(d_ff // block):
        val = conv_block(j * block)
        gate = conv_block(d_ff + j * block)
        a_scr[:, j * block:(j + 1) * block] = (_silu(gate) * val).astype(_BF16)
    y = x + _dot(a_scr[...], w_dn_ref[...])
    if final_norm:
        y = _rmsnorm(y, fg_ref[...])
    out_ref[...] = y


def _resident(shape):
    zeros = (0,) * len(shape)
    return pl.BlockSpec(shape, lambda b, i: zeros, pipeline_mode=pl.Buffered(1))


def _token_tile(seq):
    for tm in (512, 256, 128):
        if seq % tm == 0:
            return tm
    raise ValueError(f"sequence length {seq} must be a multiple of {HGRN_CHUNK}")


def _params():
    return pltpu.CompilerParams(dimension_semantics=("arbitrary", "arbitrary"),
                                vmem_limit_bytes=V7X_VMEM_LIMIT_BYTES)


def _mixer(x, n1g, w_in, bg, pw, ps, lbl, hng, w_pa, w_pb, w_o, *, layer):
    batch, seq, d_model = x.shape
    tm = _token_tile(seq)
    width = w_pa.shape[0]
    n_heads = width // HEAD_DIM
    dmat, level, n_levels = _hierarchy_tables(HGRN_CHUNK)
    dmat2 = np.concatenate([dmat, dmat], axis=1)
    consts = (n1g, w_in, bg, pw, ps, lbl, hng, w_pa, w_pb, w_o, jnp.asarray(dmat2, _BF16), jnp.asarray(level))
    tile = pl.BlockSpec((None, tm, d_model), lambda b, i: (b, i, 0))
    return pl.pallas_call(
        functools.partial(_mixer_kernel, layer=layer, n_levels=n_levels),
        out_shape=jax.ShapeDtypeStruct(x.shape, _F32),
        grid=(batch, seq // tm),
        in_specs=[tile] + [_resident(c.shape) for c in consts],
        out_specs=tile,
        scratch_shapes=[
            pltpu.VMEM((tm, w_in.shape[1]), _F32),
            pltpu.VMEM((tm, width), _F32),
            pltpu.VMEM((POOL_HISTORY, width), _F32),
            pltpu.VMEM((n_heads, HEAD_DIM, HEAD_DIM), _F32),
        ],
        compiler_params=_params(),
        name=f"mixer_l{layer}",
    )(x, *consts)


def _ffn(x, n2g, w_up, cw, cb, w_dn, fg, *, layer, final_norm):
    batch, seq, d_model = x.shape
    tm = _token_tile(seq)
    d_ff = w_dn.shape[0]
    block = 256 if d_ff % 256 == 0 else 128
    consts = (n2g, w_up, cw, cb, w_dn, fg)
    tile = pl.BlockSpec((None, tm, d_model), lambda b, i: (b, i, 0))
    return pl.pallas_call(
        functools.partial(_ffn_kernel, block=block, final_norm=final_norm),
        out_shape=jax.ShapeDtypeStruct(x.shape, _F32),
        grid=(batch, seq // tm),
        in_specs=[tile] + [_resident(c.shape) for c in consts],
        out_specs=tile,
        scratch_shapes=[
            pltpu.VMEM((tm, d_ff), _BF16),
            pltpu.VMEM((CONV_TAIL, 2 * d_ff), _F32),
        ],
        compiler_params=_params(),
        name=f"ffn_l{layer}",
    )(x, *consts)


def kernel(x, norm1_g, w_in, b_gate, pool_w, pool_scale, lb_logits, hgrn_norm_g, w_pa, w_pb, w_o, norm2_g,
           w_up, conv_w, conv_b, w_down, final_g):
    depth = w_in.shape[0]
    x = x.astype(_F32)
    lbl = lb_logits.astype(_F32)
    for l in range(depth):
        x = _mixer(x, norm1_g[l][None, :], w_in[l].astype(_BF16), b_gate[l][None, :], pool_w[l].astype(_BF16),
                   pool_scale[l][None, :], lbl, hgrn_norm_g[l][None, :], w_pa[l].astype(_BF16),
                   w_pb[l].astype(_BF16), w_o[l].astype(_BF16), layer=l)
        x = _ffn(x, norm2_g[l][None, :], w_up[l].astype(_BF16), conv_w[l], conv_b[l][None, :],
                 w_down[l].astype(_BF16), final_g[None, :], layer=l, final_norm=(l == depth - 1))
    return x
```

```python
import functools

import numpy as np
import jax
import jax.numpy as jnp
from jax import lax
from jax.experimental import pallas as pl
from jax.experimental.pallas import tpu as pltpu

EPS = 1e-6
POOL_WINDOWS = (2, 4, 8, 16)
POOL_HISTORY = 16
HEAD_DIM = 128
HGRN_CHUNK = 128
CONV_WIDTH = 3
CONV_TAIL = 8
V7X_VMEM_LIMIT_BYTES = 56 * 1024 * 1024
LOG2E = 1.4426950408889634

_F32 = jnp.float32
_BF16 = jnp.bfloat16


def _sigmoid(x):
    return 0.5 * jnp.tanh(0.5 * x) + 0.5


def _silu(x):
    return x * _sigmoid(x)


def _rmsnorm(x, g):
    return x * lax.rsqrt(jnp.mean(x * x, axis=-1, keepdims=True) + EPS) * g


def _dot(a, b):
    return jnp.dot(a, b, preferred_element_type=_F32)


def _dot_nt(a, b):
    return lax.dot_general(a, b, (((1,), (1,)), ((), ())), preferred_element_type=_F32)


def _dot_tn(a, b):
    return lax.dot_general(a, b, (((0,), (0,)), ((), ())), preferred_element_type=_F32)


def _hierarchy_tables(chunk):
    n_levels = int(np.log2(chunk))
    assert 1 << n_levels == chunk
    t = np.arange(chunk)[:, None]
    u = np.arange(chunk)[None, :]
    mats = []
    for l in range(1, n_levels):
        m = 1 << l
        p = t % (2 * m)
        upper = p >= m
        block_start = t - p + m
        block_end = t - p + m - 1
        mats.append(np.where(upper, (u >= block_start) & (u <= t), (u > t) & (u <= block_end)))
    mats.append(u <= t)
    mats.append(u > t)
    dmat = np.concatenate(mats, axis=0).astype(np.float32)
    dmat = np.concatenate([dmat, dmat], axis=1)
    xor = t ^ u
    level = np.full((chunk, chunk), n_levels + 8, np.int32)
    lower = t > u
    level[lower] = np.floor(np.log2(xor[lower])).astype(np.int32)
    level[np.arange(chunk), np.arange(chunk)] = -1
    return dmat, level, n_levels


def _mixer_kernel(x_ref, n1g_ref, w_in_ref, bg_ref, pw_ref, ps_ref, lbl_ref, hng_ref, w_pa_ref, w_pb_ref,
                  w_o_ref, dmat_ref, lvl_ref, out_ref, qb_scr, qf_scr, kb_scr, vb_scr, gs_scr, so_scr, gate_scr,
                  w_scr, dec_scr, o_scr, uh_scr, st_scr, *, layer, n_levels):
    tm, d_model = x_ref.shape
    n_heads = st_scr.shape[0]
    width = n_heads * HEAD_DIM
    chunk = HGRN_CHUNK
    n_chunks = tm // chunk
    i = pl.program_id(1)

    @pl.when(i == 0)
    def _():
        uh_scr[...] = jnp.zeros_like(uh_scr)
        st_scr[...] = jnp.zeros_like(st_scr)

    x = x_ref[...]
    xn = _rmsnorm(x, n1g_ref[...]).astype(_BF16)

    def proj(j):
        return _dot(xn, w_in_ref[:, j * width:(j + 1) * width])

    logits = lbl_ref[...]
    e = jnp.exp(logits - jnp.max(logits, axis=0, keepdims=True))
    soft = e / jnp.sum(e, axis=0, keepdims=True)
    lb = jnp.zeros((1, width), _F32)
    for r in range(1, layer + 1):
        lb = lb + soft[r:r + 1, :]

    q = _silu(proj(1))
    qb_scr[...] = q.astype(_BF16)
    f = lb + (1.0 - lb) * jnp.exp(-jnp.log(1.0 + jnp.exp(-proj(2))))
    qf_scr[...] = (q * f).astype(_BF16)
    kb_scr[...] = (1.0 - f).astype(_BF16)
    g = jnp.log(f) * LOG2E
    g_hi = g.astype(_BF16)
    g_lo = (g - g_hi.astype(_F32)).astype(_BF16)
    for c in range(n_chunks):
        gs_scr[c, 0:chunk, :] = g_hi[c * chunk:(c + 1) * chunk, :]
        gs_scr[c, chunk:2 * chunk, :] = g_lo[c * chunk:(c + 1) * chunk, :]
    vb_scr[...] = proj(3).astype(_BF16)
    so_scr[...] = _silu(proj(4))

    u = proj(0)
    ext = jnp.concatenate([uh_scr[...], u], axis=0)
    uh_scr[...] = u[tm - POOL_HISTORY:, :]
    sums = ext
    frames_seen = (i * tm + 1 + lax.broadcasted_iota(jnp.int32, (tm, HEAD_DIM), 0)).astype(_F32)
    mixed = []
    for gi, w in enumerate(POOL_WINDOWS):
        sums = sums[:, (HEAD_DIM if gi else 0):]
        sums = sums + pltpu.roll(sums, w // 2, 0)
        win = sums[POOL_HISTORY:, 0:HEAD_DIM]
        u_g = u[:, gi * HEAD_DIM:(gi + 1) * HEAD_DIM]
        pooled = win / jnp.minimum(frames_seen, float(w)) - u_g
        mixed.append(_dot(pooled.astype(_BF16), pw_ref[gi]))
    mixed = jnp.concatenate(mixed, axis=1) * ps_ref[...]
    ya = _dot(mixed.astype(_BF16), w_pa_ref[...])

    for j in range(2 * d_model // width):
        gate_scr[:, j * width:(j + 1) * width] = _sigmoid(proj(5 + j) + bg_ref[:, j * width:(j + 1) * width])

    for c in range(n_chunks):
        expo = _dot(dmat_ref[...], gs_scr[c])
        w_scr[c] = jnp.exp2(expo).astype(_BF16)
        dec_scr[c:c + 1, :] = jnp.exp2(expo[n_levels * chunk - 1:n_levels * chunk, :])

    level = lvl_ref[...]

    def tile_of(ref, c, h):
        return ref[c * chunk:(c + 1) * chunk, h * HEAD_DIM:(h + 1) * HEAD_DIM]

    scores = {}
    k_tr = {}
    for c in range(n_chunks):
        for h in range(n_heads):
            k_t = tile_of(kb_scr, c, h)
            k_tr[c, h] = k_t.T
            diag = jnp.where(level == -1, _dot_nt(tile_of(qb_scr, c, h), k_t), 0.0)
            scores[c, h] = jnp.where(level == 0, _dot_nt(tile_of(qf_scr, c, h), k_t), diag)
    for l in range(1, n_levels):
        for c in range(n_chunks):
            wgt = w_scr[c, (l - 1) * chunk:l * chunk, :]
            q_l = qb_scr[c * chunk:(c + 1) * chunk, :] * wgt
            for h in range(n_heads):
                lanes = slice(h * HEAD_DIM, (h + 1) * HEAD_DIM)
                k_l = k_tr[c, h] * wgt[:, lanes].T
                scores[c, h] = jnp.where(level == l, _dot(q_l[:, lanes], k_l), scores[c, h])

    hng = hng_ref[...]
    for c in range(n_chunks):
        rows = slice(c * chunk, (c + 1) * chunk)
        q_in = qb_scr[rows, :] * w_scr[c, (n_levels - 1) * chunk:n_levels * chunk, :]
        k_out = kb_scr[rows, :] * w_scr[c, n_levels * chunk:(n_levels + 1) * chunk, :]
        for h in range(n_heads):
            lanes = slice(h * HEAD_DIM, (h + 1) * HEAD_DIM)
            st = st_scr[h]
            v_t = vb_scr[rows, lanes]
            o = _dot(scores[c, h].astype(_BF16), v_t) + _dot_nt(q_in[:, lanes], st.astype(_BF16))
            st_scr[h] = dec_scr[c:c + 1, lanes] * st + _dot_tn(v_t, k_out[:, lanes])
            o = o * lax.rsqrt(jnp.mean(o * o, axis=-1, keepdims=True) + EPS) * hng
            o_scr[rows, lanes] = o * so_scr[rows, lanes]
    yb = _dot(o_scr[...].astype(_BF16), w_pb_ref[...])

    merged = gate_scr[:, :d_model] * ya + gate_scr[:, d_model:] * yb
    out_ref[...] = x + _dot(merged.astype(_BF16), w_o_ref[...])


def _ffn_kernel(x_ref, n2g_ref, w_up_ref, cw_ref, cb_ref, w_dn_ref, fg_ref, out_ref, a_scr, tail_scr, *,
                block, final_norm):
    tm, d_model = x_ref.shape
    d_ff = w_dn_ref.shape[0]
    i = pl.program_id(1)

    @pl.when(i == 0)
    def _():
        tail_scr[...] = jnp.zeros_like(tail_scr)

    x = x_ref[...]
    xn = _rmsnorm(x, n2g_ref[...]).astype(_BF16)
    row = lax.broadcasted_iota(jnp.int32, (tm, block), 0)

    def conv_block(c0):
        h = _dot(xn, w_up_ref[:, c0:c0 + block])
        tail = tail_scr[:, c0:c0 + block]
        tail_scr[:, c0:c0 + block] = h[tm - CONV_TAIL:, :]
        prev1 = tail[CONV_TAIL - 1:CONV_TAIL, :]
        prev2 = tail[CONV_TAIL - 2:CONV_TAIL - 1, :]
        h1 = jnp.where(row == 0, prev1, pltpu.roll(h, 1, 0))
        h2 = jnp.where(row == 0, prev2, jnp.where(row == 1, prev1, pltpu.roll(h, 2, 0)))
        cw = cw_ref[:, c0:c0 + block]
        return cw[2:3, :] * h + cw[1:2, :] * h1 + cw[0:1, :] * h2 + cb_ref[:, c0:c0 + block]

    for j in range(d_ff // block):
        val = conv_block(j * block)
        gate = conv_block(d_ff + j * block)
        a_scr[:, j * block:(j + 1) * block] = (_silu(gate) * val).astype(_BF16)
    y = x + _dot(a_scr[...], w_dn_ref[...])
    if final_norm:
        y = _rmsnorm(y, fg_ref[...])
    out_ref[...] = y


def _resident(shape):
    zeros = (0,) * len(shape)
    return pl.BlockSpec(shape, lambda b, i: zeros, pipeline_mode=pl.Buffered(1))


def _token_tile(seq):
    for tm in (512, 256, 128):
        if seq % tm == 0:
            return tm
    raise ValueError(f"sequence length {seq} must be a multiple of {HGRN_CHUNK}")


def _params():
    return pltpu.CompilerParams(dimension_semantics=("arbitrary", "arbitrary"),
                                vmem_limit_bytes=V7X_VMEM_LIMIT_BYTES)


def _mixer(x, n1g, w_in, bg, pw, ps, lbl, hng, w_pa, w_pb, w_o, *, layer):
    batch, seq, d_model = x.shape
    tm = _token_tile(seq)
    width = w_pa.shape[0]
    n_heads = width // HEAD_DIM
    dmat, level, n_levels = _hierarchy_tables(HGRN_CHUNK)
    consts = (n1g, w_in, bg, pw, ps, lbl, hng, w_pa, w_pb, w_o, jnp.asarray(dmat, _BF16), jnp.asarray(level))
    n_chunks = tm // HGRN_CHUNK
    tile = pl.BlockSpec((None, tm, d_model), lambda b, i: (b, i, 0))
    return pl.pallas_call(
        functools.partial(_mixer_kernel, layer=layer, n_levels=n_levels),
        out_shape=jax.ShapeDtypeStruct(x.shape, _F32),
        grid=(batch, seq // tm),
        in_specs=[tile] + [_resident(c.shape) for c in consts],
        out_specs=tile,
        scratch_shapes=[
            pltpu.VMEM((tm, width), _BF16),
            pltpu.VMEM((tm, width), _BF16),
            pltpu.VMEM((tm, width), _BF16),
            pltpu.VMEM((tm, width), _BF16),
            pltpu.VMEM((n_chunks, 2 * HGRN_CHUNK, width), _BF16),
            pltpu.VMEM((tm, width), _F32),
            pltpu.VMEM((tm, 2 * d_model), _F32),
            pltpu.VMEM((n_chunks, dmat.shape[0], width), _BF16),
            pltpu.VMEM((8 * ((n_chunks + 7) // 8), width), _F32),
            pltpu.VMEM((tm, width), _F32),
            pltpu.VMEM((POOL_HISTORY, width), _F32),
            pltpu.VMEM((n_heads, HEAD_DIM, HEAD_DIM), _F32),
        ],
        compiler_params=_params(),
        name=f"mixer_l{layer}",
    )(x, *consts)


def _ffn(x, n2g, w_up, cw, cb, w_dn, fg, *, layer, final_norm):
    batch, seq, d_model = x.shape
    tm = _token_tile(seq)
    d_ff = w_dn.shape[0]
    block = 256 if d_ff % 256 == 0 else 128
    consts = (n2g, w_up, cw, cb, w_dn, fg)
    tile = pl.BlockSpec((None, tm, d_model), lambda b, i: (b, i, 0))
    return pl.pallas_call(
        functools.partial(_ffn_kernel, block=block, final_norm=final_norm),
        out_shape=jax.ShapeDtypeStruct(x.shape, _F32),
        grid=(batch, seq // tm),
        in_specs=[tile] + [_resident(c.shape) for c in consts],
        out_specs=tile,
        scratch_shapes=[
            pltpu.VMEM((tm, d_ff), _BF16),
            pltpu.VMEM((CONV_TAIL, 2 * d_ff), _F32),
        ],
        compiler_params=_params(),
        name=f"ffn_l{layer}",
    )(x, *consts)


def kernel(x, norm1_g, w_in, b_gate, pool_w, pool_scale, lb_logits, hgrn_norm_g, w_pa, w_pb, w_o, norm2_g,
           w_up, conv_w, conv_b, w_down, final_g):
    depth = w_in.shape[0]
    x = x.astype(_F32)
    lbl = lb_logits.astype(_F32)
    for l in range(depth):
        x = _mixer(x, norm1_g[l][None, :], w_in[l].astype(_BF16), b_gate[l][None, :], pool_w[l].astype(_BF16),
                   pool_scale[l][None, :], lbl, hgrn_norm_g[l][None, :], w_pa[l].astype(_BF16),
                   w_pb[l].astype(_BF16), w_o[l].astype(_BF16), layer=l)
        x = _ffn(x, norm2_g[l][None, :], w_up[l].astype(_BF16), conv_w[l], conv_b[l][None, :],
                 w_down[l].astype(_BF16), final_g[None, :], layer=l, final_norm=(l == depth - 1))
    return x
```

```python
import functools

import numpy as np
import jax
import jax.numpy as jnp
from jax import lax
from jax.experimental import pallas as pl
from jax.experimental.pallas import tpu as pltpu

EPS = 1e-6
POOL_WINDOWS = (2, 4, 8, 16)
POOL_HISTORY = 16
HEAD_DIM = 128
HGRN_CHUNK = 128
CONV_WIDTH = 3
CONV_TAIL = 8
V7X_VMEM_LIMIT_BYTES = 56 * 1024 * 1024
PROJ_BLOCK = 256
MIXER_TILE = 512
FFN_TILE = 1024
LOG2E = 1.4426950408889634

_F32 = jnp.float32
_BF16 = jnp.bfloat16


def _sigmoid(x):
    return 0.5 * jnp.tanh(0.5 * x) + 0.5


def _silu(x):
    h = 0.5 * x
    return h * jnp.tanh(h) + h


def _rmsnorm(x, g):
    return x * lax.rsqrt(jnp.mean(x * x, axis=-1, keepdims=True) + EPS) * g


def _dot(a, b):
    return jnp.dot(a, b, preferred_element_type=_F32)


def _dot_nt(a, b):
    return lax.dot_general(a, b, (((1,), (1,)), ((), ())), preferred_element_type=_F32)


def _dot_tn(a, b):
    return lax.dot_general(a, b, (((0,), (0,)), ((), ())), preferred_element_type=_F32)


def _hierarchy_tables(chunk):
    n_levels = int(np.log2(chunk))
    assert 1 << n_levels == chunk
    t = np.arange(chunk)[:, None]
    u = np.arange(chunk)[None, :]
    mats = []
    for l in range(1, n_levels):
        m = 1 << l
        p = t % (2 * m)
        upper = p >= m
        block_start = t - p + m
        block_end = t - p + m - 1
        mats.append(np.where(upper, (u >= block_start) & (u <= t), (u > t) & (u <= block_end)))
    mats.append(u <= t)
    mats.append(u > t)
    dmat = np.concatenate(mats, axis=0).astype(np.float32)
    dmat = np.concatenate([dmat, dmat], axis=1)
    xor = t ^ u
    level = np.full((chunk, chunk), n_levels + 8, np.int32)
    lower = t > u
    level[lower] = np.floor(np.log2(xor[lower])).astype(np.int32)
    level[np.arange(chunk), np.arange(chunk)] = -1
    return dmat, level, n_levels


def _mixer_kernel(x_ref, n1g_ref, w_in_ref, bg_ref, pw_ref, ps_ref, lbl_ref, hng_ref, w_pa_ref, w_pb_ref,
                  w_o_ref, dmat_ref, lvl_ref, out_ref, qb_scr, qf_scr, kb_scr, vb_scr, gs_scr, so_scr, gate_scr,
                  w_scr, dec_scr, o_scr, uh_scr, st_scr, *, layer, n_levels):
    tm, d_model = x_ref.shape
    n_heads = st_scr.shape[0]
    width = n_heads * HEAD_DIM
    chunk = HGRN_CHUNK
    n_chunks = tm // chunk
    i = pl.program_id(1)

    @pl.when(i == 0)
    def _():
        uh_scr[...] = jnp.zeros_like(uh_scr)
        st_scr[...] = jnp.zeros_like(st_scr)

    x = x_ref[...]
    xn = _rmsnorm(x, n1g_ref[...]).astype(_BF16)

    def proj(c0, n):
        return _dot(xn, w_in_ref[:, c0:c0 + n])

    logits = lbl_ref[...]
    e = jnp.exp(logits - jnp.max(logits, axis=0, keepdims=True))
    soft = e / jnp.sum(e, axis=0, keepdims=True)
    lb = jnp.zeros((1, width), _F32)
    for r in range(1, layer + 1):
        lb = lb + soft[r:r + 1, :]

    for c0 in range(0, width, PROJ_BLOCK):
        cols = slice(c0, c0 + PROJ_BLOCK)
        q = _silu(proj(width + c0, PROJ_BLOCK))
        qb_scr[:, cols] = q.astype(_BF16)
        sig = jnp.exp(-jnp.log(1.0 + jnp.exp(-proj(2 * width + c0, PROJ_BLOCK))))
        f = lb[:, cols] + (1.0 - lb[:, cols]) * sig
        qf_scr[:, cols] = (q * f).astype(_BF16)
        kb_scr[:, cols] = (1.0 - f).astype(_BF16)
        g = jnp.log(f) * LOG2E
        g_hi = g.astype(_BF16)
        g_lo = (g - g_hi.astype(_F32)).astype(_BF16)
        for c in range(n_chunks):
            gs_scr[c, 0:chunk, cols] = g_hi[c * chunk:(c + 1) * chunk, :]
            gs_scr[c, chunk:2 * chunk, cols] = g_lo[c * chunk:(c + 1) * chunk, :]
        vb_scr[:, cols] = proj(3 * width + c0, PROJ_BLOCK).astype(_BF16)
        so_scr[:, cols] = _silu(proj(4 * width + c0, PROJ_BLOCK))

    u = jnp.concatenate([proj(c0, PROJ_BLOCK) for c0 in range(0, width, PROJ_BLOCK)], axis=1)
    ext = jnp.concatenate([uh_scr[...], u], axis=0)
    uh_scr[...] = u[tm - POOL_HISTORY:, :]
    head_frames = (i * tm + 1 + lax.broadcasted_iota(jnp.int32, (POOL_HISTORY, HEAD_DIM), 0)).astype(_F32)
    gate_starts = list(range(0, 2 * d_model, PROJ_BLOCK))
    gates_per_group = -(-len(gate_starts) // len(POOL_WINDOWS))
    sums = ext
    mixed = []
    for gi, w in enumerate(POOL_WINDOWS):
        sums = sums[:, (HEAD_DIM if gi else 0):]
        sums = sums + pltpu.roll(sums, w // 2, 0)
        win = sums[POOL_HISTORY:, 0:HEAD_DIM]
        u_g = u[:, gi * HEAD_DIM:(gi + 1) * HEAD_DIM]
        inv_count = jnp.concatenate([1.0 / jnp.minimum(head_frames, float(w)),
                                     jnp.full((tm - POOL_HISTORY, HEAD_DIM), 1.0 / w, _F32)], axis=0)
        pooled = win * inv_count - u_g
        mixed.append(_dot(pooled.astype(_BF16), pw_ref[gi]))
        for c0 in gate_starts[gi * gates_per_group:(gi + 1) * gates_per_group]:
            cols = slice(c0, c0 + PROJ_BLOCK)
            gate_scr[:, cols] = _sigmoid(proj(5 * width + c0, PROJ_BLOCK) + bg_ref[:, cols])
    mixed = jnp.concatenate(mixed, axis=1) * ps_ref[...]
    ya = _dot(mixed.astype(_BF16), w_pa_ref[...])

    level = lvl_ref[...]

    def tile_of(ref, c, h):
        return ref[c * chunk:(c + 1) * chunk, h * HEAD_DIM:(h + 1) * HEAD_DIM]

    def decay_weights(c):
        expo = _dot(dmat_ref[...], gs_scr[c])
        wgt = jnp.exp2(expo).astype(_BF16)
        w_scr[c] = wgt
        dec_scr[c:c + 1, :] = jnp.exp2(expo[n_levels * chunk - 1:n_levels * chunk, :])
        return {(l, h): wgt[(l - 1) * chunk:l * chunk, h * HEAD_DIM:(h + 1) * HEAD_DIM].T
                for l in range(1, n_levels) for h in range(n_heads)}

    scores = {}

    def level_scores(c, w_tr):
        k_tr = {}
        for h in range(n_heads):
            k_t = tile_of(kb_scr, c, h)
            k_tr[h] = k_t.T
            diag = jnp.where(level == -1, _dot_nt(tile_of(qb_scr, c, h), k_t), 0.0)
            scores[c, h] = jnp.where(level == 0, _dot_nt(tile_of(qf_scr, c, h), k_t), diag)
        for l in range(1, n_levels):
            q_l = qb_scr[c * chunk:(c + 1) * chunk, :] * w_scr[c, (l - 1) * chunk:l * chunk, :]
            for h in range(n_heads):
                lanes = slice(h * HEAD_DIM, (h + 1) * HEAD_DIM)
                scores[c, h] = jnp.where(level == l, _dot(q_l[:, lanes], k_tr[h] * w_tr[l, h]), scores[c, h])

    pending = None
    for c in range(n_chunks):
        w_tr = decay_weights(c)
        if pending is not None:
            level_scores(*pending)
        pending = (c, w_tr)
    level_scores(*pending)

    hng = hng_ref[...]
    for c in range(n_chunks):
        rows = slice(c * chunk, (c + 1) * chunk)
        q_in = qb_scr[rows, :] * w_scr[c, (n_levels - 1) * chunk:n_levels * chunk, :]
        k_out = kb_scr[rows, :] * w_scr[c, n_levels * chunk:(n_levels + 1) * chunk, :]
        for h in range(n_heads):
            lanes = slice(h * HEAD_DIM, (h + 1) * HEAD_DIM)
            st = st_scr[h]
            v_t = vb_scr[rows, lanes]
            o = _dot(scores[c, h].astype(_BF16), v_t) + _dot_nt(q_in[:, lanes], st.astype(_BF16))
            st_scr[h] = dec_scr[c:c + 1, lanes] * st + _dot_tn(v_t, k_out[:, lanes])
            o = o * lax.rsqrt(jnp.mean(o * o, axis=-1, keepdims=True) + EPS) * hng
            o_scr[rows, lanes] = o * so_scr[rows, lanes]
    yb = _dot(o_scr[...].astype(_BF16), w_pb_ref[...])

    merged = gate_scr[:, :d_model] * ya + gate_scr[:, d_model:] * yb
    out_ref[...] = x + _dot(merged.astype(_BF16), w_o_ref[...])


def _ffn_kernel(x_ref, n2g_ref, w_up_ref, cw_ref, cb_ref, w_dn_ref, fg_ref, out_ref, a_scr, tail_scr, *,
                block, final_norm):
    tm, d_model = x_ref.shape
    d_ff = w_dn_ref.shape[0]
    i = pl.program_id(1)

    @pl.when(i == 0)
    def _():
        tail_scr[...] = jnp.zeros_like(tail_scr)

    x = x_ref[...]
    xn = _rmsnorm(x, n2g_ref[...]).astype(_BF16)

    def conv_block(c0, scale):
        h = _dot(xn, w_up_ref[:, c0:c0 + block])
        ext = jnp.concatenate([tail_scr[:, c0:c0 + block], h], axis=0)
        tail_scr[:, c0:c0 + block] = h[tm - CONV_TAIL:, :]
        h1 = pltpu.roll(ext, 1, 0)[CONV_TAIL:, :]
        h2 = pltpu.roll(ext, 2, 0)[CONV_TAIL:, :]
        cw = cw_ref[:, c0:c0 + block] * scale
        return cw[2:3, :] * h + cw[1:2, :] * h1 + cw[0:1, :] * h2 + cb_ref[:, c0:c0 + block] * scale

    for j in range(d_ff // block):
        val = conv_block(j * block, 1.0)
        half_gate = conv_block(d_ff + j * block, 0.5)
        a_scr[:, j * block:(j + 1) * block] = ((half_gate * jnp.tanh(half_gate) + half_gate) * val).astype(_BF16)
    y = x + _dot(a_scr[...], w_dn_ref[...])
    if final_norm:
        y = _rmsnorm(y, fg_ref[...])
    out_ref[...] = y


def _resident(shape):
    zeros = (0,) * len(shape)
    return pl.BlockSpec(shape, lambda b, i: zeros, pipeline_mode=pl.Buffered(1))


def _layer_of(stacked, layer):
    tail = (0,) * (stacked.ndim - 1)
    return pl.BlockSpec((None,) + stacked.shape[1:], lambda b, i: (layer,) + tail, pipeline_mode=pl.Buffered(1))


def _token_tile(seq, largest):
    tm = largest
    while tm >= HGRN_CHUNK:
        if seq % tm == 0:
            return tm
        tm //= 2
    raise ValueError(f"sequence length {seq} must be a multiple of {HGRN_CHUNK}")


def _params():
    return pltpu.CompilerParams(dimension_semantics=("arbitrary", "arbitrary"),
                                vmem_limit_bytes=V7X_VMEM_LIMIT_BYTES)


def _mixer(x, stacked, lbl, *, layer):
    batch, seq, d_model = x.shape
    tm = _token_tile(seq, MIXER_TILE)
    n1g, w_in, bg, pw, ps, hng, w_pa, w_pb, w_o = stacked
    width = w_pa.shape[1]
    n_heads = width // HEAD_DIM
    dmat, level, n_levels = _hierarchy_tables(HGRN_CHUNK)
    dmat, level = jnp.asarray(dmat, _BF16), jnp.asarray(level)
    n_chunks = tm // HGRN_CHUNK
    tile = pl.BlockSpec((None, tm, d_model), lambda b, i: (b, i, 0))
    per_layer = functools.partial(_layer_of, layer=layer)
    return pl.pallas_call(
        functools.partial(_mixer_kernel, layer=layer, n_levels=n_levels),
        out_shape=jax.ShapeDtypeStruct(x.shape, _F32),
        grid=(batch, seq // tm),
        in_specs=[tile, per_layer(n1g), per_layer(w_in), per_layer(bg), per_layer(pw), per_layer(ps),
                  _resident(lbl.shape), per_layer(hng), per_layer(w_pa), per_layer(w_pb), per_layer(w_o),
                  _resident(dmat.shape), _resident(level.shape)],
        out_specs=tile,
        scratch_shapes=[
            pltpu.VMEM((tm, width), _BF16),
            pltpu.VMEM((tm, width), _BF16),
            pltpu.VMEM((tm, width), _BF16),
            pltpu.VMEM((tm, width), _BF16),
            pltpu.VMEM((n_chunks, 2 * HGRN_CHUNK, width), _BF16),
            pltpu.VMEM((tm, width), _F32),
            pltpu.VMEM((tm, 2 * d_model), _F32),
            pltpu.VMEM((n_chunks, dmat.shape[0], width), _BF16),
            pltpu.VMEM((8 * ((n_chunks + 7) // 8), width), _F32),
            pltpu.VMEM((tm, width), _F32),
            pltpu.VMEM((POOL_HISTORY, width), _F32),
            pltpu.VMEM((n_heads, HEAD_DIM, HEAD_DIM), _F32),
        ],
        compiler_params=_params(),
        name=f"mixer_l{layer}",
    )(x, n1g, w_in, bg, pw, ps, lbl, hng, w_pa, w_pb, w_o, dmat, level)


def _ffn(x, stacked, fg, *, layer, final_norm):
    batch, seq, d_model = x.shape
    tm = _token_tile(seq, FFN_TILE)
    n2g, w_up, cw, cb, w_dn = stacked
    d_ff = w_dn.shape[1]
    block = 256 if d_ff % 256 == 0 else 128
    tile = pl.BlockSpec((None, tm, d_model), lambda b, i: (b, i, 0))
    per_layer = functools.partial(_layer_of, layer=layer)
    return pl.pallas_call(
        functools.partial(_ffn_kernel, block=block, final_norm=final_norm),
        out_shape=jax.ShapeDtypeStruct(x.shape, _F32),
        grid=(batch, seq // tm),
        in_specs=[tile] + [per_layer(a) for a in stacked] + [_resident(fg.shape)],
        out_specs=tile,
        scratch_shapes=[
            pltpu.VMEM((tm, d_ff), _BF16),
            pltpu.VMEM((CONV_TAIL, 2 * d_ff), _F32),
        ],
        compiler_params=_params(),
        name=f"ffn_l{layer}",
    )(x, *stacked, fg)


def kernel(x, norm1_g, w_in, b_gate, pool_w, pool_scale, lb_logits, hgrn_norm_g, w_pa, w_pb, w_o, norm2_g,
           w_up, conv_w, conv_b, w_down, final_g):
    depth = w_in.shape[0]
    x = x.astype(_F32)

    def row(a):
        return a.astype(_F32)[:, None, :]

    mixer_params = (row(norm1_g), w_in.astype(_BF16), row(b_gate), pool_w.astype(_BF16), row(pool_scale),
                    row(hgrn_norm_g), w_pa.astype(_BF16), w_pb.astype(_BF16), w_o.astype(_BF16))
    ffn_params = (row(norm2_g), w_up.astype(_BF16), conv_w.astype(_F32), row(conv_b), w_down.astype(_BF16))
    lbl = lb_logits.astype(_F32)
    fg = final_g.astype(_F32)[None, :]
    for l in range(depth):
        x = _mixer(x, mixer_params, lbl, layer=l)
        x = _ffn(x, ffn_params, fg, layer=l, final_norm=(l == depth - 1))
    return x
```

```python
import functools

import numpy as np
import jax
import jax.numpy as jnp
from jax import lax
from jax.experimental import pallas as pl
from jax.experimental.pallas import tpu as pltpu

EPS = 1e-6
POOL_WINDOWS = (2, 4, 8, 16)
POOL_HISTORY = 16
HEAD_DIM = 128
HGRN_CHUNK = 128
CONV_WIDTH = 3
CONV_TAIL = 8
V7X_VMEM_LIMIT_BYTES = 56 * 1024 * 1024
PROJ_BLOCK = 256
MIXER_TILE = 512
FFN_TILE = 1024
LOG2E = 1.4426950408889634
ROW_PACK = 16
MATMUL_LEVELS = 3

_F32 = jnp.float32
_BF16 = jnp.bfloat16


def _sigmoid(x):
    return 0.5 * jnp.tanh(0.5 * x) + 0.5


def _silu(x):
    h = 0.5 * x
    return h * jnp.tanh(h) + h


def _rmsnorm(x, g):
    return x * lax.rsqrt(jnp.mean(x * x, axis=-1, keepdims=True) + EPS) * g


def _dot(a, b):
    return jnp.dot(a, b, preferred_element_type=_F32)


def _dot_nt(a, b):
    return lax.dot_general(a, b, (((1,), (1,)), ((), ())), preferred_element_type=_F32)


def _dot_tn(a, b):
    return lax.dot_general(a, b, (((0,), (0,)), ((), ())), preferred_element_type=_F32)


def _hierarchy_tables(chunk):
    n_levels = int(np.log2(chunk))
    assert 1 << n_levels == chunk
    t = np.arange(chunk)[:, None]
    u = np.arange(chunk)[None, :]
    mats = []
    for l in range(1, MATMUL_LEVELS):
        m = 1 << l
        p = t % (2 * m)
        upper = p >= m
        block_start = t - p + m
        block_end = t - p + m - 1
        mats.append(np.where(upper, (u >= block_start) & (u <= t), (u > t) & (u <= block_end)))
    mats.append(u <= t)
    dmat = np.concatenate(mats, axis=0).astype(np.float32)
    dmat = np.concatenate([dmat, dmat], axis=1)
    xor = t ^ u
    level = np.full((chunk, chunk), n_levels + 8, np.int32)
    lower = t > u
    level[lower] = np.floor(np.log2(xor[lower])).astype(np.int32)
    level[np.arange(chunk), np.arange(chunk)] = -1
    return dmat, level, n_levels


def _mixer_kernel(x_ref, n1g_ref, w_in_ref, bg_ref, pw_ref, ps_ref, lbl_ref, hng_ref, w_pa_ref, w_pb_ref,
                  w_o_ref, dmat_ref, lvl_ref, out_ref, qb_scr, qf_scr, kb_scr, vb_scr, gs_scr, so_scr, gate_scr,
                  w_scr, dec_scr, o_scr, uh_scr, st_scr, *, layer, n_levels):
    tm, d_model = x_ref.shape
    n_heads = st_scr.shape[0]
    width = n_heads * HEAD_DIM
    chunk = HGRN_CHUNK
    n_chunks = tm // chunk
    i = pl.program_id(1)

    @pl.when(i == 0)
    def _():
        uh_scr[...] = jnp.zeros_like(uh_scr)
        st_scr[...] = jnp.zeros_like(st_scr)

    x = x_ref[...]
    xn = _rmsnorm(x, n1g_ref[...]).astype(_BF16)

    def proj(c0, n):
        return _dot(xn, w_in_ref[:, c0:c0 + n])

    logits = lbl_ref[...]
    e = jnp.exp(logits - jnp.max(logits, axis=0, keepdims=True))
    soft = e / jnp.sum(e, axis=0, keepdims=True)
    lb = jnp.zeros((1, width), _F32)
    for r in range(1, layer + 1):
        lb = lb + soft[r:r + 1, :]

    for c0 in range(0, width, PROJ_BLOCK):
        cols = slice(c0, c0 + PROJ_BLOCK)
        q = _silu(proj(width + c0, PROJ_BLOCK))
        qb_scr[:, cols] = q.astype(_BF16)
        sig = jnp.exp(-jnp.log(1.0 + jnp.exp(-proj(2 * width + c0, PROJ_BLOCK))))
        f = lb[:, cols] + (1.0 - lb[:, cols]) * sig
        qf_scr[:, cols] = (q * f).astype(_BF16)
        kb_scr[:, cols] = (1.0 - f).astype(_BF16)
        g = jnp.log(f) * LOG2E
        g_hi = g.astype(_BF16)
        g_lo = (g - g_hi.astype(_F32)).astype(_BF16)
        for c in range(n_chunks):
            gs_scr[c, 0:chunk, cols] = g_hi[c * chunk:(c + 1) * chunk, :]
            gs_scr[c, chunk:2 * chunk, cols] = g_lo[c * chunk:(c + 1) * chunk, :]
        vb_scr[:, cols] = proj(3 * width + c0, PROJ_BLOCK).astype(_BF16)
        so_scr[:, cols] = _silu(proj(4 * width + c0, PROJ_BLOCK))

    u = jnp.concatenate([proj(c0, PROJ_BLOCK) for c0 in range(0, width, PROJ_BLOCK)], axis=1)
    ext = jnp.concatenate([uh_scr[...], u], axis=0)
    uh_scr[...] = u[tm - POOL_HISTORY:, :]
    head_frames = (i * tm + 1 + lax.broadcasted_iota(jnp.int32, (POOL_HISTORY, HEAD_DIM), 0)).astype(_F32)
    gate_starts = list(range(0, 2 * d_model, PROJ_BLOCK))
    gates_per_group = -(-len(gate_starts) // len(POOL_WINDOWS))
    sums = ext
    mixed = []
    for gi, w in enumerate(POOL_WINDOWS):
        sums = sums[:, (HEAD_DIM if gi else 0):]
        sums = sums + pltpu.roll(sums, w // 2, 0)
        win = sums[POOL_HISTORY:, 0:HEAD_DIM]
        u_g = u[:, gi * HEAD_DIM:(gi + 1) * HEAD_DIM]
        inv_count = jnp.concatenate([1.0 / jnp.minimum(head_frames, float(w)),
                                     jnp.full((tm - POOL_HISTORY, HEAD_DIM), 1.0 / w, _F32)], axis=0)
        pooled = win * inv_count - u_g
        mixed.append(_dot(pooled.astype(_BF16), pw_ref[gi]))
        for c0 in gate_starts[gi * gates_per_group:(gi + 1) * gates_per_group]:
            cols = slice(c0, c0 + PROJ_BLOCK)
            gate_scr[:, cols] = _sigmoid(proj(5 * width + c0, PROJ_BLOCK) + bg_ref[:, cols])
    mixed = jnp.concatenate(mixed, axis=1) * ps_ref[...]
    ya = _dot(mixed.astype(_BF16), w_pa_ref[...])

    level = lvl_ref[...]

    def tile_of(ref, c, h):
        return ref[c * chunk:(c + 1) * chunk, h * HEAD_DIM:(h + 1) * HEAD_DIM]

    def decay_weights(c):
        expo = _dot(dmat_ref[...], gs_scr[c])
        n_mm = (MATMUL_LEVELS - 1) * chunk
        b = expo[n_mm:, :]
        parts = [expo[:n_mm, :]]
        for l in range(MATMUL_LEVELS, n_levels):
            m = 1 << l
            for g0 in range(0, chunk, 2 * m):
                ref = jnp.broadcast_to(b[g0 + m - 1:g0 + m, :], (m, width))
                parts.append(ref - b[g0:g0 + m, :])
                parts.append(b[g0 + m:g0 + 2 * m, :] - ref)
        parts.append(b)
        parts.append(jnp.broadcast_to(b[chunk - 1:chunk, :], (chunk, width)) - b)
        wgt = jnp.exp2(jnp.concatenate(parts, axis=0)).astype(_BF16)
        w_scr[c] = wgt
        dec_scr[c:c + 1, :] = jnp.exp2(b[chunk - 1:chunk, :])
        return {(l, h): wgt[(l - 1) * chunk:l * chunk, h * HEAD_DIM:(h + 1) * HEAD_DIM].T
                for l in range(1, n_levels) for h in range(n_heads)}

    scores = {}

    def level_scores(c, w_tr):
        k_tr = {}
        for h in range(n_heads):
            k_t = tile_of(kb_scr, c, h)
            k_tr[h] = k_t.T
            diag = jnp.where(level == -1, _dot_nt(tile_of(qb_scr, c, h), k_t), 0.0)
            scores[c, h] = jnp.where(level == 0, _dot_nt(tile_of(qf_scr, c, h), k_t), diag)
        for l in range(1, n_levels):
            m = 1 << l
            wgt = w_scr[c, (l - 1) * chunk:l * chunk, :]
            q_c = qb_scr[c * chunk:(c + 1) * chunk, :]
            if m < ROW_PACK:
                q_l = q_c * wgt
                for h in range(n_heads):
                    lanes = slice(h * HEAD_DIM, (h + 1) * HEAD_DIM)
                    scores[c, h] = jnp.where(level == l, _dot(q_l[:, lanes], k_tr[h] * w_tr[l, h]), scores[c, h])
                continue
            upper = [slice(g0 + m, g0 + 2 * m) for g0 in range(0, chunk, 2 * m)]
            q_l = jnp.concatenate([q_c[r, :] * wgt[r, :] for r in upper], axis=0)
            for h in range(n_heads):
                lanes = slice(h * HEAD_DIM, (h + 1) * HEAD_DIM)
                part = _dot(q_l[:, lanes], k_tr[h] * w_tr[l, h])
                old = scores[c, h]
                pieces = []
                for gi, r in enumerate(upper):
                    pieces.append(old[r.start - m:r.start, :])
                    pieces.append(jnp.where(level[r, :] == l, part[gi * m:(gi + 1) * m, :], old[r, :]))
                scores[c, h] = jnp.concatenate(pieces, axis=0)

    pending = None
    for c in range(n_chunks):
        w_tr = decay_weights(c)
        if pending is not None:
            level_scores(*pending)
        pending = (c, w_tr)
    level_scores(*pending)

    hng = hng_ref[...]
    for c in range(n_chunks):
        rows = slice(c * chunk, (c + 1) * chunk)
        q_in = qb_scr[rows, :] * w_scr[c, (n_levels - 1) * chunk:n_levels * chunk, :]
        k_out = kb_scr[rows, :] * w_scr[c, n_levels * chunk:(n_levels + 1) * chunk, :]
        for h in range(n_heads):
            lanes = slice(h * HEAD_DIM, (h + 1) * HEAD_DIM)
            st = st_scr[h]
            v_t = vb_scr[rows, lanes]
            o = _dot(scores[c, h].astype(_BF16), v_t) + _dot_nt(q_in[:, lanes], st.astype(_BF16))
            st_scr[h] = dec_scr[c:c + 1, lanes] * st + _dot_tn(v_t, k_out[:, lanes])
            o = o * lax.rsqrt(jnp.mean(o * o, axis=-1, keepdims=True) + EPS) * hng
            o_scr[rows, lanes] = o * so_scr[rows, lanes]
    yb = _dot(o_scr[...].astype(_BF16), w_pb_ref[...])

    merged = gate_scr[:, :d_model] * ya + gate_scr[:, d_model:] * yb
    out_ref[...] = x + _dot(merged.astype(_BF16), w_o_ref[...])


def _ffn_kernel(x_ref, n2g_ref, w_up_ref, cw_ref, cb_ref, w_dn_ref, fg_ref, out_ref, a_scr, tail_scr, *,
                block, final_norm):
    tm, d_model = x_ref.shape
    d_ff = w_dn_ref.shape[0]
    i = pl.program_id(1)

    @pl.when(i == 0)
    def _():
        tail_scr[...] = jnp.zeros_like(tail_scr)

    x = x_ref[...]
    xn = _rmsnorm(x, n2g_ref[...]).astype(_BF16)

    def conv_block(c0, scale):
        h = _dot(xn, w_up_ref[:, c0:c0 + block])
        ext = jnp.concatenate([tail_scr[:, c0:c0 + block], h], axis=0)
        tail_scr[:, c0:c0 + block] = h[tm - CONV_TAIL:, :]
        h1 = pltpu.roll(ext, 1, 0)[CONV_TAIL:, :]
        h2 = pltpu.roll(ext, 2, 0)[CONV_TAIL:, :]
        cw = cw_ref[:, c0:c0 + block] * scale
        return cw[2:3, :] * h + cw[1:2, :] * h1 + cw[0:1, :] * h2 + cb_ref[:, c0:c0 + block] * scale

    for j in range(d_ff // block):
        val = conv_block(j * block, 1.0)
        half_gate = conv_block(d_ff + j * block, 0.5)
        a_scr[:, j * block:(j + 1) * block] = ((half_gate * jnp.tanh(half_gate) + half_gate) * val).astype(_BF16)
    y = x + _dot(a_scr[...], w_dn_ref[...])
    if final_norm:
        y = _rmsnorm(y, fg_ref[...])
    out_ref[...] = y


def _resident(shape):
    zeros = (0,) * len(shape)
    return pl.BlockSpec(shape, lambda b, i: zeros, pipeline_mode=pl.Buffered(1))


def _layer_of(stacked, layer):
    tail = (0,) * (stacked.ndim - 1)
    return pl.BlockSpec((None,) + stacked.shape[1:], lambda b, i: (layer,) + tail, pipeline_mode=pl.Buffered(1))


def _token_tile(seq, largest):
    tm = largest
    while tm >= HGRN_CHUNK:
        if seq % tm == 0:
            return tm
        tm //= 2
    raise ValueError(f"sequence length {seq} must be a multiple of {HGRN_CHUNK}")


def _params():
    return pltpu.CompilerParams(dimension_semantics=("arbitrary", "arbitrary"),
                                vmem_limit_bytes=V7X_VMEM_LIMIT_BYTES)


def _mixer(x, stacked, lbl, *, layer):
    batch, seq, d_model = x.shape
    tm = _token_tile(seq, MIXER_TILE)
    n1g, w_in, bg, pw, ps, hng, w_pa, w_pb, w_o = stacked
    width = w_pa.shape[1]
    n_heads = width // HEAD_DIM
    dmat, level, n_levels = _hierarchy_tables(HGRN_CHUNK)
    dmat, level = jnp.asarray(dmat, _BF16), jnp.asarray(level)
    n_chunks = tm // HGRN_CHUNK
    tile = pl.BlockSpec((None, tm, d_model), lambda b, i: (b, i, 0))
    per_layer = functools.partial(_layer_of, layer=layer)
    return pl.pallas_call(
        functools.partial(_mixer_kernel, layer=layer, n_levels=n_levels),
        out_shape=jax.ShapeDtypeStruct(x.shape, _F32),
        grid=(batch, seq // tm),
        in_specs=[tile, per_layer(n1g), per_layer(w_in), per_layer(bg), per_layer(pw), per_layer(ps),
                  _resident(lbl.shape), per_layer(hng), per_layer(w_pa), per_layer(w_pb), per_layer(w_o),
                  _resident(dmat.shape), _resident(level.shape)],
        out_specs=tile,
        scratch_shapes=[
            pltpu.VMEM((tm, width), _BF16),
            pltpu.VMEM((tm, width), _BF16),
            pltpu.VMEM((tm, width), _BF16),
            pltpu.VMEM((tm, width), _BF16),
            pltpu.VMEM((n_chunks, 2 * HGRN_CHUNK, width), _BF16),
            pltpu.VMEM((tm, width), _F32),
            pltpu.VMEM((tm, 2 * d_model), _F32),
            pltpu.VMEM((n_chunks, (n_levels + 1) * HGRN_CHUNK, width), _BF16),
            pltpu.VMEM((8 * ((n_chunks + 7) // 8), width), _F32),
            pltpu.VMEM((tm, width), _F32),
            pltpu.VMEM((POOL_HISTORY, width), _F32),
            pltpu.VMEM((n_heads, HEAD_DIM, HEAD_DIM), _F32),
        ],
        compiler_params=_params(),
        name=f"mixer_l{layer}",
    )(x, n1g, w_in, bg, pw, ps, lbl, hng, w_pa, w_pb, w_o, dmat, level)


def _ffn(x, stacked, fg, *, layer, final_norm):
    batch, seq, d_model = x.shape
    tm = _token_tile(seq, FFN_TILE)
    n2g, w_up, cw, cb, w_dn = stacked
    d_ff = w_dn.shape[1]
    block = 256 if d_ff % 256 == 0 else 128
    tile = pl.BlockSpec((None, tm, d_model), lambda b, i: (b, i, 0))
    per_layer = functools.partial(_layer_of, layer=layer)
    return pl.pallas_call(
        functools.partial(_ffn_kernel, block=block, final_norm=final_norm),
        out_shape=jax.ShapeDtypeStruct(x.shape, _F32),
        grid=(batch, seq // tm),
        in_specs=[tile] + [per_layer(a) for a in stacked] + [_resident(fg.shape)],
        out_specs=tile,
        scratch_shapes=[
            pltpu.VMEM((tm, d_ff), _BF16),
            pltpu.VMEM((CONV_TAIL, 2 * d_ff), _F32),
        ],
        compiler_params=_params(),
        name=f"ffn_l{layer}",
    )(x, *stacked, fg)


def kernel(x, norm1_g, w_in, b_gate, pool_w, pool_scale, lb_logits, hgrn_norm_g, w_pa, w_pb, w_o, norm2_g,
           w_up, conv_w, conv_b, w_down, final_g):
    depth = w_in.shape[0]
    x = x.astype(_F32)

    def row(a):
        return a.astype(_F32)[:, None, :]

    mixer_params = (row(norm1_g), w_in.astype(_BF16), row(b_gate), pool_w.astype(_BF16), row(pool_scale),
                    row(hgrn_norm_g), w_pa.astype(_BF16), w_pb.astype(_BF16), w_o.astype(_BF16))
    ffn_params = (row(norm2_g), w_up.astype(_BF16), conv_w.astype(_F32), row(conv_b), w_down.astype(_BF16))
    lbl = lb_logits.astype(_F32)
    fg = final_g.astype(_F32)[None, :]
    for l in range(depth):
        x = _mixer(x, mixer_params, lbl, layer=l)
        x = _ffn(x, ffn_params, fg, layer=l, final_norm=(l == depth - 1))
    return x
```

```python
import functools

import numpy as np
import jax
import jax.numpy as jnp
from jax import lax
from jax.experimental import pallas as pl
from jax.experimental.pallas import tpu as pltpu

EPS = 1e-6
POOL_WINDOWS = (2, 4, 8, 16)
POOL_HISTORY = 16
HEAD_DIM = 128
HGRN_CHUNK = 128
CONV_WIDTH = 3
CONV_TAIL = 8
V7X_VMEM_LIMIT_BYTES = 56 * 1024 * 1024
PROJ_BLOCK = 256
MIXER_TILE = 512
FFN_TILE = 1024
LOG2E = 1.4426950408889634
ROW_PACK = 16
MATMUL_LEVELS = 3

_F32 = jnp.float32
_BF16 = jnp.bfloat16


def _sigmoid(x):
    return 0.5 * jnp.tanh(0.5 * x) + 0.5


def _silu(x):
    h = 0.5 * x
    return h * jnp.tanh(h) + h


def _rmsnorm(x, g):
    return x * lax.rsqrt(jnp.mean(x * x, axis=-1, keepdims=True) + EPS) * g


def _dot(a, b):
    return jnp.dot(a, b, preferred_element_type=_F32)


def _dot_nt(a, b):
    return lax.dot_general(a, b, (((1,), (1,)), ((), ())), preferred_element_type=_F32)


def _dot_tn(a, b):
    return lax.dot_general(a, b, (((0,), (0,)), ((), ())), preferred_element_type=_F32)


def _hierarchy_tables(chunk):
    n_levels = int(np.log2(chunk))
    assert 1 << n_levels == chunk
    t = np.arange(chunk)[:, None]
    u = np.arange(chunk)[None, :]
    mats = []
    for l in range(1, MATMUL_LEVELS):
        m = 1 << l
        p = t % (2 * m)
        upper = p >= m
        block_start = t - p + m
        block_end = t - p + m - 1
        mats.append(np.where(upper, (u >= block_start) & (u <= t), (u > t) & (u <= block_end)))
    mats.append(u <= t)
    dmat = np.concatenate(mats, axis=0).astype(np.float32)
    dmat = np.concatenate([dmat, dmat], axis=1)
    xor = t ^ u
    level = np.full((chunk, chunk), n_levels + 8, np.int32)
    lower = t > u
    level[lower] = np.floor(np.log2(xor[lower])).astype(np.int32)
    level[np.arange(chunk), np.arange(chunk)] = -1
    return dmat, level, n_levels


def _mixer_kernel(x_ref, n1g_ref, w_in_ref, bg_ref, pw_ref, ps_ref, lbl_ref, hng_ref, w_pa_ref, w_pb_ref,
                  w_o_ref, dmat_ref, lvl_ref, out_ref, qb_scr, qf_scr, kb_scr, vb_scr, gs_scr, so_scr, gate_scr,
                  w_scr, dec_scr, o_scr, uh_scr, st_scr, *, layer, n_levels):
    tm, d_model = x_ref.shape
    n_heads = st_scr.shape[0]
    width = n_heads * HEAD_DIM
    chunk = HGRN_CHUNK
    n_chunks = tm // chunk
    i = pl.program_id(1)

    @pl.when(i == 0)
    def _():
        uh_scr[...] = jnp.zeros_like(uh_scr)
        st_scr[...] = jnp.zeros_like(st_scr)

    x = x_ref[...]
    xn = _rmsnorm(x, n1g_ref[...]).astype(_BF16)

    def proj(c0, n):
        return _dot(xn, w_in_ref[:, c0:c0 + n])

    logits = lbl_ref[...]
    e = jnp.exp(logits - jnp.max(logits, axis=0, keepdims=True))
    soft = e / jnp.sum(e, axis=0, keepdims=True)
    lb = jnp.zeros((1, width), _F32)
    for r in range(1, layer + 1):
        lb = lb + soft[r:r + 1, :]

    for c0 in range(0, width, PROJ_BLOCK):
        cols = slice(c0, c0 + PROJ_BLOCK)
        q = _silu(proj(width + c0, PROJ_BLOCK))
        qb_scr[:, cols] = q.astype(_BF16)
        sig = jnp.exp(-jnp.log(1.0 + jnp.exp(-proj(2 * width + c0, PROJ_BLOCK))))
        f = lb[:, cols] + (1.0 - lb[:, cols]) * sig
        qf_scr[:, cols] = (q * f).astype(_BF16)
        kb_scr[:, cols] = (1.0 - f).astype(_BF16)
        g = jnp.log(f) * LOG2E
        g_hi = g.astype(_BF16)
        g_lo = (g - g_hi.astype(_F32)).astype(_BF16)
        for c in range(n_chunks):
            gs_scr[c, 0:chunk, cols] = g_hi[c * chunk:(c + 1) * chunk, :]
            gs_scr[c, chunk:2 * chunk, cols] = g_lo[c * chunk:(c + 1) * chunk, :]
        vb_scr[:, cols] = proj(3 * width + c0, PROJ_BLOCK).astype(_BF16)
        so_scr[:, cols] = _silu(proj(4 * width + c0, PROJ_BLOCK))

    u = jnp.concatenate([proj(c0, PROJ_BLOCK) for c0 in range(0, width, PROJ_BLOCK)], axis=1)
    ext = jnp.concatenate([uh_scr[...], u], axis=0)
    uh_scr[...] = u[tm - POOL_HISTORY:, :]
    head_frames = (i * tm + 1 + lax.broadcasted_iota(jnp.int32, (POOL_HISTORY, HEAD_DIM), 0)).astype(_F32)
    gate_starts = list(range(0, 2 * d_model, PROJ_BLOCK))
    gates_per_group = -(-len(gate_starts) // len(POOL_WINDOWS))
    sums = ext
    mixed = []
    for gi, w in enumerate(POOL_WINDOWS):
        sums = sums[:, (HEAD_DIM if gi else 0):]
        sums = sums + pltpu.roll(sums, w // 2, 0)
        win = sums[POOL_HISTORY:, 0:HEAD_DIM]
        u_g = u[:, gi * HEAD_DIM:(gi + 1) * HEAD_DIM]
        inv_count = jnp.concatenate([1.0 / jnp.minimum(head_frames, float(w)),
                                     jnp.full((tm - POOL_HISTORY, HEAD_DIM), 1.0 / w, _F32)], axis=0)
        pooled = win * inv_count - u_g
        mixed.append(_dot(pooled.astype(_BF16), pw_ref[gi]))
        for c0 in gate_starts[gi * gates_per_group:(gi + 1) * gates_per_group]:
            cols = slice(c0, c0 + PROJ_BLOCK)
            gate_scr[:, cols] = _sigmoid(proj(5 * width + c0, PROJ_BLOCK) + bg_ref[:, cols])
    mixed = jnp.concatenate(mixed, axis=1) * ps_ref[...]
    ya = _dot(mixed.astype(_BF16), w_pa_ref[...])

    level = lvl_ref[...]

    def tile_of(ref, c, h):
        return ref[c * chunk:(c + 1) * chunk, h * HEAD_DIM:(h + 1) * HEAD_DIM]

    def decay_weights(c):
        expo = _dot(dmat_ref[...], gs_scr[c])
        n_mm = (MATMUL_LEVELS - 1) * chunk
        b = expo[n_mm:, :]
        parts = [expo[:n_mm, :]]
        for l in range(MATMUL_LEVELS, n_levels):
            m = 1 << l
            for g0 in range(0, chunk, 2 * m):
                ref = jnp.broadcast_to(b[g0 + m - 1:g0 + m, :], (m, width))
                parts.append(ref - b[g0:g0 + m, :])
                parts.append(b[g0 + m:g0 + 2 * m, :] - ref)
        parts.append(b)
        parts.append(jnp.broadcast_to(b[chunk - 1:chunk, :], (chunk, width)) - b)
        wgt = jnp.exp2(jnp.concatenate(parts, axis=0)).astype(_BF16)
        w_scr[c] = wgt
        dec_scr[c:c + 1, :] = jnp.exp2(b[chunk - 1:chunk, :])
        return {(l, h): wgt[(l - 1) * chunk:l * chunk, h * HEAD_DIM:(h + 1) * HEAD_DIM].T
                for l in range(1, n_levels) for h in range(n_heads)}

    scores = {}

    def level_scores(c, w_tr):
        k_tr = {}
        for h in range(n_heads):
            k_t = tile_of(kb_scr, c, h)
            k_tr[h] = k_t.T
            diag = jnp.where(level == -1, _dot_nt(tile_of(qb_scr, c, h), k_t), 0.0)
            scores[c, h] = jnp.where(level == 0, _dot_nt(tile_of(qf_scr, c, h), k_t), diag)
        for l in range(1, n_levels):
            m = 1 << l
            wgt = w_scr[c, (l - 1) * chunk:l * chunk, :]
            q_c = qb_scr[c * chunk:(c + 1) * chunk, :]
            if m < ROW_PACK:
                q_l = q_c * wgt
                for h in range(n_heads):
                    lanes = slice(h * HEAD_DIM, (h + 1) * HEAD_DIM)
                    scores[c, h] = jnp.where(level == l, _dot(q_l[:, lanes], k_tr[h] * w_tr[l, h]), scores[c, h])
                continue
            upper = [slice(g0 + m, g0 + 2 * m) for g0 in range(0, chunk, 2 * m)]
            q_l = jnp.concatenate([q_c[r, :] * wgt[r, :] for r in upper], axis=0)
            for h in range(n_heads):
                lanes = slice(h * HEAD_DIM, (h + 1) * HEAD_DIM)
                part = _dot(q_l[:, lanes], k_tr[h] * w_tr[l, h])
                old = scores[c, h]
                pieces = []
                for gi, r in enumerate(upper):
                    pieces.append(old[r.start - m:r.start, :])
                    pieces.append(jnp.where(level[r, :] == l, part[gi * m:(gi + 1) * m, :], old[r, :]))
                scores[c, h] = jnp.concatenate(pieces, axis=0)

    pending = None
    for c in range(n_chunks):
        w_tr = decay_weights(c)
        if pending is not None:
            level_scores(*pending)
        pending = (c, w_tr)
    level_scores(*pending)

    hng = hng_ref[...]
    for c in range(n_chunks):
        rows = slice(c * chunk, (c + 1) * chunk)
        q_in = qb_scr[rows, :] * w_scr[c, (n_levels - 1) * chunk:n_levels * chunk, :]
        k_out = kb_scr[rows, :] * w_scr[c, n_levels * chunk:(n_levels + 1) * chunk, :]
        for h in range(n_heads):
            lanes = slice(h * HEAD_DIM, (h + 1) * HEAD_DIM)
            st = st_scr[h]
            v_t = vb_scr[rows, lanes]
            o = _dot(scores[c, h].astype(_BF16), v_t) + _dot_nt(q_in[:, lanes], st.astype(_BF16))
            st_scr[h] = dec_scr[c:c + 1, lanes] * st + _dot_tn(v_t, k_out[:, lanes])
            o = o * lax.rsqrt(jnp.mean(o * o, axis=-1, keepdims=True) + EPS) * hng
            o_scr[rows, lanes] = o * so_scr[rows, lanes]
    yb = _dot(o_scr[...].astype(_BF16), w_pb_ref[...])

    merged = gate_scr[:, :d_model] * ya + gate_scr[:, d_model:] * yb
    out_ref[...] = x + _dot(merged.astype(_BF16), w_o_ref[...])


def _ffn_kernel(x_ref, n2g_ref, w_up_ref, cw_ref, cb_ref, w_dn_ref, fg_ref, out_ref, a_scr, tail_scr, *,
                block, final_norm):
    tm, d_model = x_ref.shape
    d_ff = w_dn_ref.shape[0]
    i = pl.program_id(1)

    @pl.when(i == 0)
    def _():
        tail_scr[...] = jnp.zeros_like(tail_scr)

    x = x_ref[...]
    xn = _rmsnorm(x, n2g_ref[...]).astype(_BF16)

    def conv_block(c0, scale):
        h = _dot(xn, w_up_ref[:, c0:c0 + block])
        ext = jnp.concatenate([tail_scr[:, c0:c0 + block], h], axis=0)
        tail_scr[:, c0:c0 + block] = h[tm - CONV_TAIL:, :]
        h1 = pltpu.roll(ext, 1, 0)[CONV_TAIL:, :]
        h2 = pltpu.roll(ext, 2, 0)[CONV_TAIL:, :]
        cw = cw_ref[:, c0:c0 + block] * scale
        return cw[2:3, :] * h + cw[1:2, :] * h1 + cw[0:1, :] * h2 + cb_ref[:, c0:c0 + block] * scale

    for j in range(d_ff // block):
        val = conv_block(j * block, 1.0)
        half_gate = conv_block(d_ff + j * block, 0.5)
        a_scr[:, j * block:(j + 1) * block] = ((half_gate * jnp.tanh(half_gate) + half_gate) * val).astype(_BF16)
    y = x + _dot(a_scr[...], w_dn_ref[...])
    if final_norm:
        y = _rmsnorm(y, fg_ref[...])
    out_ref[...] = y


def _resident(shape):
    zeros = (0,) * len(shape)
    return pl.BlockSpec(shape, lambda b, i: zeros, pipeline_mode=pl.Buffered(1))


def _layer_of(stacked, layer):
    tail = (0,) * (stacked.ndim - 1)
    return pl.BlockSpec((None,) + stacked.shape[1:], lambda b, i: (layer,) + tail, pipeline_mode=pl.Buffered(1))


def _token_tile(seq, largest):
    tm = largest
    while tm >= HGRN_CHUNK:
        if seq % tm == 0:
            return tm
        tm //= 2
    raise ValueError(f"sequence length {seq} must be a multiple of {HGRN_CHUNK}")


def _params():
    return pltpu.CompilerParams(dimension_semantics=("arbitrary", "arbitrary"),
                                vmem_limit_bytes=V7X_VMEM_LIMIT_BYTES)


def _cast_plan(stacked, layer, n_outer, n_inner):
    steps = n_outer * n_inner
    rows = stacked.shape[1]
    n_blocks = 1
    for cand in range(steps, 0, -1):
        if steps % cand == 0 and rows % cand == 0 and (rows // cand) % ROW_PACK == 0:
            n_blocks = cand
            break
    per_block = steps // n_blocks
    block_rows = rows // n_blocks
    src = pl.BlockSpec((None, block_rows, stacked.shape[2]), lambda b, i: (layer, (b * n_inner + i) // per_block, 0))
    dst = pl.BlockSpec((block_rows, stacked.shape[2]), lambda b, i: ((b * n_inner + i) // per_block, 0))
    return src, dst, jax.ShapeDtypeStruct(stacked.shape[1:], _BF16)


def _with_side_casts(body, n_in, n_cast):
    def kernel_fn(*refs):
        inputs = refs[:n_in]
        sources = refs[n_in:n_in + n_cast]
        out_ref = refs[n_in + n_cast]
        targets = refs[n_in + n_cast + 1:n_in + 2 * n_cast + 1]
        scratch = refs[n_in + 2 * n_cast + 1:]
        for src, dst in zip(sources, targets):
            dst[...] = src[...].astype(dst.dtype)
        body(*inputs, out_ref, *scratch)
    return kernel_fn


def _mixer(x, small, weights, lbl, *, layer, convert=()):
    batch, seq, d_model = x.shape
    tm = _token_tile(seq, MIXER_TILE)
    n1g, bg, pw, ps, hng = small
    w_in, w_pa, w_pb, w_o = weights
    width = w_pa.shape[0]
    n_heads = width // HEAD_DIM
    dmat, level, n_levels = _hierarchy_tables(HGRN_CHUNK)
    dmat, level = jnp.asarray(dmat, _BF16), jnp.asarray(level)
    n_chunks = tm // HGRN_CHUNK
    grid = (batch, seq // tm)
    tile = pl.BlockSpec((None, tm, d_model), lambda b, i: (b, i, 0))
    per_layer = functools.partial(_layer_of, layer=layer)
    plans = [_cast_plan(w, layer, *grid) for w in convert]
    inputs = (x, n1g, w_in, bg, pw, ps, lbl, hng, w_pa, w_pb, w_o, dmat, level)
    in_specs = [tile, per_layer(n1g), _resident(w_in.shape), per_layer(bg), per_layer(pw), per_layer(ps),
                _resident(lbl.shape), per_layer(hng), _resident(w_pa.shape), _resident(w_pb.shape),
                _resident(w_o.shape), _resident(dmat.shape), _resident(level.shape)]
    body = functools.partial(_mixer_kernel, layer=layer, n_levels=n_levels)
    return pl.pallas_call(
        _with_side_casts(body, len(inputs), len(convert)),
        out_shape=[jax.ShapeDtypeStruct(x.shape, _F32)] + [p[2] for p in plans],
        grid=grid,
        in_specs=in_specs + [p[0] for p in plans],
        out_specs=[tile] + [p[1] for p in plans],
        scratch_shapes=[
            pltpu.VMEM((tm, width), _BF16),
            pltpu.VMEM((tm, width), _BF16),
            pltpu.VMEM((tm, width), _BF16),
            pltpu.VMEM((tm, width), _BF16),
            pltpu.VMEM((n_chunks, 2 * HGRN_CHUNK, width), _BF16),
            pltpu.VMEM((tm, width), _F32),
            pltpu.VMEM((tm, 2 * d_model), _F32),
            pltpu.VMEM((n_chunks, (n_levels + 1) * HGRN_CHUNK, width), _BF16),
            pltpu.VMEM((8 * ((n_chunks + 7) // 8), width), _F32),
            pltpu.VMEM((tm, width), _F32),
            pltpu.VMEM((POOL_HISTORY, width), _F32),
            pltpu.VMEM((n_heads, HEAD_DIM, HEAD_DIM), _F32),
        ],
        compiler_params=_params(),
        name=f"mixer_l{layer}",
    )(*inputs, *convert)


def _ffn(x, small, weights, fg, *, layer, final_norm, convert=(), convert_layer=0):
    batch, seq, d_model = x.shape
    tm = _token_tile(seq, FFN_TILE)
    n2g, cw, cb = small
    w_up, w_dn = weights
    d_ff = w_dn.shape[0]
    block = 256 if d_ff % 256 == 0 else 128
    grid = (batch, seq // tm)
    tile = pl.BlockSpec((None, tm, d_model), lambda b, i: (b, i, 0))
    per_layer = functools.partial(_layer_of, layer=layer)
    plans = [_cast_plan(w, convert_layer, *grid) for w in convert]
    inputs = (x, n2g, w_up, cw, cb, w_dn, fg)
    in_specs = [tile, per_layer(n2g), _resident(w_up.shape), per_layer(cw), per_layer(cb), _resident(w_dn.shape),
                _resident(fg.shape)]
    body = functools.partial(_ffn_kernel, block=block, final_norm=final_norm)
    return pl.pallas_call(
        _with_side_casts(body, len(inputs), len(convert)),
        out_shape=[jax.ShapeDtypeStruct(x.shape, _F32)] + [p[2] for p in plans],
        grid=grid,
        in_specs=in_specs + [p[0] for p in plans],
        out_specs=[tile] + [p[1] for p in plans],
        scratch_shapes=[
            pltpu.VMEM((tm, d_ff), _BF16),
            pltpu.VMEM((CONV_TAIL, 2 * d_ff), _F32),
        ],
        compiler_params=_params(),
        name=f"ffn_l{layer}",
    )(*inputs, *convert)


def kernel(x, norm1_g, w_in, b_gate, pool_w, pool_scale, lb_logits, hgrn_norm_g, w_pa, w_pb, w_o, norm2_g,
           w_up, conv_w, conv_b, w_down, final_g):
    depth = w_in.shape[0]
    x = x.astype(_F32)

    def row(a):
        return a.astype(_F32)[:, None, :]

    mixer_small = (row(norm1_g), row(b_gate), pool_w.astype(_BF16), row(pool_scale), row(hgrn_norm_g))
    ffn_small = (row(norm2_g), conv_w.astype(_F32), row(conv_b))
    mixer_f32 = tuple(w.astype(_F32) for w in (w_in, w_pa, w_pb, w_o))
    ffn_f32 = tuple(w.astype(_F32) for w in (w_up, w_down))
    lbl = lb_logits.astype(_F32)
    fg = final_g.astype(_F32)[None, :]
    mixer_w = tuple(w[0].astype(_BF16) for w in mixer_f32)
    for l in range(depth):
        x, *ffn_w = _mixer(x, mixer_small, mixer_w, lbl, layer=l, convert=ffn_f32)
        last = l == depth - 1
        x, *mixer_w = _ffn(x, ffn_small, ffn_w, fg, layer=l, final_norm=last,
                           convert=() if last else mixer_f32, convert_layer=l + 1)
    return x
```

```python
import functools

import numpy as np
import jax
import jax.numpy as jnp
from jax import lax
from jax.experimental import pallas as pl
from jax.experimental.pallas import tpu as pltpu

EPS = 1e-6
POOL_WINDOWS = (2, 4, 8, 16)
POOL_HISTORY = 16
HEAD_DIM = 128
HGRN_CHUNK = 128
CONV_WIDTH = 3
CONV_TAIL = 8
V7X_VMEM_LIMIT_BYTES = 56 * 1024 * 1024
PROJ_BLOCK = 256
MIXER_TILE = 1024
FFN_TILE = 1024
LOG2E = 1.4426950408889634
ROW_PACK = 16
MATMUL_LEVELS = 3

_F32 = jnp.float32
_BF16 = jnp.bfloat16


def _sigmoid(x):
    return 0.5 * jnp.tanh(0.5 * x) + 0.5


def _silu(x):
    h = 0.5 * x
    return h * jnp.tanh(h) + h


def _rmsnorm(x, g):
    return x * lax.rsqrt(jnp.mean(x * x, axis=-1, keepdims=True) + EPS) * g


def _dot(a, b):
    return jnp.dot(a, b, preferred_element_type=_F32)


def _dot_nt(a, b):
    return lax.dot_general(a, b, (((1,), (1,)), ((), ())), preferred_element_type=_F32)


def _dot_tn(a, b):
    return lax.dot_general(a, b, (((0,), (0,)), ((), ())), preferred_element_type=_F32)


def _hierarchy_tables(chunk):
    n_levels = int(np.log2(chunk))
    assert 1 << n_levels == chunk
    t = np.arange(chunk)[:, None]
    u = np.arange(chunk)[None, :]
    mats = []
    for l in range(1, MATMUL_LEVELS):
        m = 1 << l
        p = t % (2 * m)
        upper = p >= m
        block_start = t - p + m
        block_end = t - p + m - 1
        mats.append(np.where(upper, (u >= block_start) & (u <= t), (u > t) & (u <= block_end)))
    mats.append(u <= t)
    dmat = np.concatenate(mats, axis=0).astype(np.float32)
    dmat = np.concatenate([dmat, dmat], axis=1)
    xor = t ^ u
    level = np.full((chunk, chunk), n_levels + 8, np.int32)
    lower = t > u
    level[lower] = np.floor(np.log2(xor[lower])).astype(np.int32)
    level[np.arange(chunk), np.arange(chunk)] = -1
    return dmat, level, n_levels


def _mixer_kernel(x_ref, n1g_ref, w_in_ref, bg_ref, pw_ref, ps_ref, lbl_ref, hng_ref, w_pa_ref, w_pb_ref,
                  w_o_ref, dmat_ref, lvl_ref, out_ref, qb_scr, qf_scr, kb_scr, vb_scr, gs_scr, so_scr, gate_scr,
                  w_scr, dec_scr, o_scr, uh_scr, st_scr, *, layer, n_levels):
    tm, d_model = x_ref.shape
    n_heads = st_scr.shape[0]
    width = n_heads * HEAD_DIM
    chunk = HGRN_CHUNK
    n_chunks = tm // chunk
    i = pl.program_id(1)

    @pl.when(i == 0)
    def _():
        uh_scr[...] = jnp.zeros_like(uh_scr)
        st_scr[...] = jnp.zeros_like(st_scr)

    x = x_ref[...]
    xn = _rmsnorm(x, n1g_ref[...]).astype(_BF16)

    def proj(c0, n):
        return _dot(xn, w_in_ref[:, c0:c0 + n])

    logits = lbl_ref[...]
    e = jnp.exp(logits - jnp.max(logits, axis=0, keepdims=True))
    soft = e / jnp.sum(e, axis=0, keepdims=True)
    lb = jnp.zeros((1, width), _F32)
    for r in range(1, layer + 1):
        lb = lb + soft[r:r + 1, :]

    for c0 in range(0, width, PROJ_BLOCK):
        cols = slice(c0, c0 + PROJ_BLOCK)
        q = _silu(proj(width + c0, PROJ_BLOCK))
        qb_scr[:, cols] = q.astype(_BF16)
        sig = jnp.exp(-jnp.log(1.0 + jnp.exp(-proj(2 * width + c0, PROJ_BLOCK))))
        f = lb[:, cols] + (1.0 - lb[:, cols]) * sig
        qf_scr[:, cols] = (q * f).astype(_BF16)
        kb_scr[:, cols] = (1.0 - f).astype(_BF16)
        g = jnp.log(f) * LOG2E
        g_hi = g.astype(_BF16)
        g_lo = (g - g_hi.astype(_F32)).astype(_BF16)
        for c in range(n_chunks):
            gs_scr[c, 0:chunk, cols] = g_hi[c * chunk:(c + 1) * chunk, :]
            gs_scr[c, chunk:2 * chunk, cols] = g_lo[c * chunk:(c + 1) * chunk, :]
        vb_scr[:, cols] = proj(3 * width + c0, PROJ_BLOCK).astype(_BF16)
        so_scr[:, cols] = _silu(proj(4 * width + c0, PROJ_BLOCK)).astype(_BF16)

    u = jnp.concatenate([proj(c0, PROJ_BLOCK) for c0 in range(0, width, PROJ_BLOCK)], axis=1)
    ext = jnp.concatenate([uh_scr[...], u], axis=0)
    uh_scr[...] = u[tm - POOL_HISTORY:, :]
    head_frames = (i * tm + 1 + lax.broadcasted_iota(jnp.int32, (POOL_HISTORY, HEAD_DIM), 0)).astype(_F32)
    gate_starts = list(range(0, 2 * d_model, PROJ_BLOCK))
    gates_per_group = -(-len(gate_starts) // len(POOL_WINDOWS))
    sums = ext
    mixed = []
    for gi, w in enumerate(POOL_WINDOWS):
        sums = sums[:, (HEAD_DIM if gi else 0):]
        sums = sums + pltpu.roll(sums, w // 2, 0)
        win = sums[POOL_HISTORY:, 0:HEAD_DIM]
        u_g = u[:, gi * HEAD_DIM:(gi + 1) * HEAD_DIM]
        inv_count = jnp.concatenate([1.0 / jnp.minimum(head_frames, float(w)),
                                     jnp.full((tm - POOL_HISTORY, HEAD_DIM), 1.0 / w, _F32)], axis=0)
        pooled = win * inv_count - u_g
        mixed.append(_dot(pooled.astype(_BF16), pw_ref[gi]))
        for c0 in gate_starts[gi * gates_per_group:(gi + 1) * gates_per_group]:
            cols = slice(c0, c0 + PROJ_BLOCK)
            gate_scr[:, cols] = _sigmoid(proj(5 * width + c0, PROJ_BLOCK) + bg_ref[:, cols]).astype(_BF16)
    mixed = jnp.concatenate(mixed, axis=1) * ps_ref[...]
    ya = _dot(mixed.astype(_BF16), w_pa_ref[...])

    level = lvl_ref[...]

    def tile_of(ref, c, h):
        return ref[c * chunk:(c + 1) * chunk, h * HEAD_DIM:(h + 1) * HEAD_DIM]

    def decay_weights(c):
        expo = _dot(dmat_ref[...], gs_scr[c])
        n_mm = (MATMUL_LEVELS - 1) * chunk
        b = expo[n_mm:, :]
        parts = [expo[:n_mm, :]]
        for l in range(MATMUL_LEVELS, n_levels):
            m = 1 << l
            for g0 in range(0, chunk, 2 * m):
                ref = jnp.broadcast_to(b[g0 + m - 1:g0 + m, :], (m, width))
                parts.append(ref - b[g0:g0 + m, :])
                parts.append(b[g0 + m:g0 + 2 * m, :] - ref)
        parts.append(b)
        parts.append(jnp.broadcast_to(b[chunk - 1:chunk, :], (chunk, width)) - b)
        wgt = jnp.exp2(jnp.concatenate(parts, axis=0)).astype(_BF16)
        w_scr[c] = wgt
        dec_scr[c:c + 1, :] = jnp.exp2(b[chunk - 1:chunk, :])
        return {(l, h): wgt[(l - 1) * chunk:l * chunk, h * HEAD_DIM:(h + 1) * HEAD_DIM].T
                for l in range(1, n_levels) for h in range(n_heads)}

    scores = {}

    def level_scores(c, w_tr):
        k_tr = {}
        for h in range(n_heads):
            k_t = tile_of(kb_scr, c, h)
            k_tr[h] = k_t.T
            diag = jnp.where(level == -1, _dot_nt(tile_of(qb_scr, c, h), k_t), 0.0)
            scores[c, h] = jnp.where(level == 0, _dot_nt(tile_of(qf_scr, c, h), k_t), diag)
        for l in range(1, n_levels):
            m = 1 << l
            wgt = w_scr[c, (l - 1) * chunk:l * chunk, :]
            q_c = qb_scr[c * chunk:(c + 1) * chunk, :]
            if m < ROW_PACK:
                q_l = q_c * wgt
                for h in range(n_heads):
                    lanes = slice(h * HEAD_DIM, (h + 1) * HEAD_DIM)
                    scores[c, h] = jnp.where(level == l, _dot(q_l[:, lanes], k_tr[h] * w_tr[l, h]), scores[c, h])
                continue
            upper = [slice(g0 + m, g0 + 2 * m) for g0 in range(0, chunk, 2 * m)]
            q_l = jnp.concatenate([q_c[r, :] * wgt[r, :] for r in upper], axis=0)
            for h in range(n_heads):
                lanes = slice(h * HEAD_DIM, (h + 1) * HEAD_DIM)
                part = _dot(q_l[:, lanes], k_tr[h] * w_tr[l, h])
                old = scores[c, h]
                pieces = []
                for gi, r in enumerate(upper):
                    pieces.append(old[r.start - m:r.start, :])
                    pieces.append(jnp.where(level[r, :] == l, part[gi * m:(gi + 1) * m, :], old[r, :]))
                scores[c, h] = jnp.concatenate(pieces, axis=0)

    pending = None
    for c in range(n_chunks):
        w_tr = decay_weights(c)
        if pending is not None:
            level_scores(*pending)
        pending = (c, w_tr)
    level_scores(*pending)

    hng = hng_ref[...]
    for c in range(n_chunks):
        rows = slice(c * chunk, (c + 1) * chunk)
        q_in = qb_scr[rows, :] * w_scr[c, (n_levels - 1) * chunk:n_levels * chunk, :]
        k_out = kb_scr[rows, :] * w_scr[c, n_levels * chunk:(n_levels + 1) * chunk, :]
        for h in range(n_heads):
            lanes = slice(h * HEAD_DIM, (h + 1) * HEAD_DIM)
            st = st_scr[h]
            v_t = vb_scr[rows, lanes]
            o = _dot(scores[c, h].astype(_BF16), v_t) + _dot_nt(q_in[:, lanes], st.astype(_BF16))
            st_scr[h] = dec_scr[c:c + 1, lanes] * st + _dot_tn(v_t, k_out[:, lanes])
            o = o * lax.rsqrt(jnp.mean(o * o, axis=-1, keepdims=True) + EPS) * hng
            o_scr[rows, lanes] = (o * so_scr[rows, lanes]).astype(_BF16)
    yb = _dot(o_scr[...], w_pb_ref[...])

    merged = gate_scr[:, :d_model] * ya + gate_scr[:, d_model:] * yb
    out_ref[...] = x + _dot(merged.astype(_BF16), w_o_ref[...])


def _ffn_kernel(x_ref, n2g_ref, w_up_ref, cw_ref, cb_ref, w_dn_ref, fg_ref, out_ref, a_scr, tail_scr, *,
                block, final_norm):
    tm, d_model = x_ref.shape
    d_ff = w_dn_ref.shape[0]
    i = pl.program_id(1)

    @pl.when(i == 0)
    def _():
        tail_scr[...] = jnp.zeros_like(tail_scr)

    x = x_ref[...]
    xn = _rmsnorm(x, n2g_ref[...]).astype(_BF16)

    def conv_block(c0, scale):
        h = _dot(xn, w_up_ref[:, c0:c0 + block])
        ext = jnp.concatenate([tail_scr[:, c0:c0 + block], h], axis=0)
        tail_scr[:, c0:c0 + block] = h[tm - CONV_TAIL:, :]
        h1 = pltpu.roll(ext, 1, 0)[CONV_TAIL:, :]
        h2 = pltpu.roll(ext, 2, 0)[CONV_TAIL:, :]
        cw = cw_ref[:, c0:c0 + block] * scale
        return cw[2:3, :] * h + cw[1:2, :] * h1 + cw[0:1, :] * h2 + cb_ref[:, c0:c0 + block] * scale

    for j in range(d_ff // block):
        val = conv_block(j * block, 1.0)
        half_gate = conv_block(d_ff + j * block, 0.5)
        a_scr[:, j * block:(j + 1) * block] = ((half_gate * jnp.tanh(half_gate) + half_gate) * val).astype(_BF16)
    y = x + _dot(a_scr[...], w_dn_ref[...])
    if final_norm:
        y = _rmsnorm(y, fg_ref[...])
    out_ref[...] = y


def _resident(shape):
    zeros = (0,) * len(shape)
    return pl.BlockSpec(shape, lambda b, i: zeros, pipeline_mode=pl.Buffered(1))


def _layer_of(stacked, layer):
    tail = (0,) * (stacked.ndim - 1)
    return pl.BlockSpec((None,) + stacked.shape[1:], lambda b, i: (layer,) + tail, pipeline_mode=pl.Buffered(1))


def _token_tile(seq, largest):
    tm = largest
    while tm >= HGRN_CHUNK:
        if seq % tm == 0:
            return tm
        tm //= 2
    raise ValueError(f"sequence length {seq} must be a multiple of {HGRN_CHUNK}")


def _params():
    return pltpu.CompilerParams(dimension_semantics=("arbitrary", "arbitrary"),
                                vmem_limit_bytes=V7X_VMEM_LIMIT_BYTES)


def _cast_plan(stacked, layer, n_outer, n_inner):
    steps = n_outer * n_inner
    rows = stacked.shape[1]
    n_blocks = 1
    for cand in range(steps, 0, -1):
        if steps % cand == 0 and rows % cand == 0 and (rows // cand) % ROW_PACK == 0:
            n_blocks = cand
            break
    per_block = steps // n_blocks
    block_rows = rows // n_blocks
    src = pl.BlockSpec((None, block_rows, stacked.shape[2]), lambda b, i: (layer, (b * n_inner + i) // per_block, 0))
    dst = pl.BlockSpec((block_rows, stacked.shape[2]), lambda b, i: ((b * n_inner + i) // per_block, 0))
    return src, dst, jax.ShapeDtypeStruct(stacked.shape[1:], _BF16)


def _with_side_casts(body, n_in, n_cast):
    def kernel_fn(*refs):
        inputs = refs[:n_in]
        sources = refs[n_in:n_in + n_cast]
        out_ref = refs[n_in + n_cast]
        targets = refs[n_in + n_cast + 1:n_in + 2 * n_cast + 1]
        scratch = refs[n_in + 2 * n_cast + 1:]
        for src, dst in zip(sources, targets):
            dst[...] = src[...].astype(dst.dtype)
        body(*inputs, out_ref, *scratch)
    return kernel_fn


def _mixer(x, small, weights, lbl, *, layer, convert=()):
    batch, seq, d_model = x.shape
    tm = _token_tile(seq, MIXER_TILE)
    n1g, bg, pw, ps, hng = small
    w_in, w_pa, w_pb, w_o = weights
    width = w_pa.shape[0]
    n_heads = width // HEAD_DIM
    dmat, level, n_levels = _hierarchy_tables(HGRN_CHUNK)
    dmat, level = jnp.asarray(dmat, _BF16), jnp.asarray(level)
    n_chunks = tm // HGRN_CHUNK
    grid = (batch, seq // tm)
    tile = pl.BlockSpec((None, tm, d_model), lambda b, i: (b, i, 0))
    per_layer = functools.partial(_layer_of, layer=layer)
    plans = [_cast_plan(w, layer, *grid) for w in convert]
    inputs = (x, n1g, w_in, bg, pw, ps, lbl, hng, w_pa, w_pb, w_o, dmat, level)
    in_specs = [tile, per_layer(n1g), _resident(w_in.shape), per_layer(bg), per_layer(pw), per_layer(ps),
                _resident(lbl.shape), per_layer(hng), _resident(w_pa.shape), _resident(w_pb.shape),
                _resident(w_o.shape), _resident(dmat.shape), _resident(level.shape)]
    body = functools.partial(_mixer_kernel, layer=layer, n_levels=n_levels)
    return pl.pallas_call(
        _with_side_casts(body, len(inputs), len(convert)),
        out_shape=[jax.ShapeDtypeStruct(x.shape, _F32)] + [p[2] for p in plans],
        grid=grid,
        in_specs=in_specs + [p[0] for p in plans],
        out_specs=[tile] + [p[1] for p in plans],
        scratch_shapes=[
            pltpu.VMEM((tm, width), _BF16),
            pltpu.VMEM((tm, width), _BF16),
            pltpu.VMEM((tm, width), _BF16),
            pltpu.VMEM((tm, width), _BF16),
            pltpu.VMEM((n_chunks, 2 * HGRN_CHUNK, width), _BF16),
            pltpu.VMEM((tm, width), _BF16),
            pltpu.VMEM((tm, 2 * d_model), _BF16),
            pltpu.VMEM((n_chunks, (n_levels + 1) * HGRN_CHUNK, width), _BF16),
            pltpu.VMEM((8 * ((n_chunks + 7) // 8), width), _F32),
            pltpu.VMEM((tm, width), _BF16),
            pltpu.VMEM((POOL_HISTORY, width), _F32),
            pltpu.VMEM((n_heads, HEAD_DIM, HEAD_DIM), _F32),
        ],
        compiler_params=_params(),
        name=f"mixer_l{layer}",
    )(*inputs, *convert)


def _ffn(x, small, weights, fg, *, layer, final_norm, convert=(), convert_layer=0):
    batch, seq, d_model = x.shape
    tm = _token_tile(seq, FFN_TILE)
    n2g, cw, cb = small
    w_up, w_dn = weights
    d_ff = w_dn.shape[0]
    block = 256 if d_ff % 256 == 0 else 128
    grid = (batch, seq // tm)
    tile = pl.BlockSpec((None, tm, d_model), lambda b, i: (b, i, 0))
    per_layer = functools.partial(_layer_of, layer=layer)
    plans = [_cast_plan(w, convert_layer, *grid) for w in convert]
    inputs = (x, n2g, w_up, cw, cb, w_dn, fg)
    in_specs = [tile, per_layer(n2g), _resident(w_up.shape), per_layer(cw), per_layer(cb), _resident(w_dn.shape),
                _resident(fg.shape)]
    body = functools.partial(_ffn_kernel, block=block, final_norm=final_norm)
    return pl.pallas_call(
        _with_side_casts(body, len(inputs), len(convert)),
        out_shape=[jax.ShapeDtypeStruct(x.shape, _F32)] + [p[2] for p in plans],
        grid=grid,
        in_specs=in_specs + [p[0] for p in plans],
        out_specs=[tile] + [p[1] for p in plans],
        scratch_shapes=[
            pltpu.VMEM((tm, d_ff), _BF16),
            pltpu.VMEM((CONV_TAIL, 2 * d_ff), _F32),
        ],
        compiler_params=_params(),
        name=f"ffn_l{layer}",
    )(*inputs, *convert)


def kernel(x, norm1_g, w_in, b_gate, pool_w, pool_scale, lb_logits, hgrn_norm_g, w_pa, w_pb, w_o, norm2_g,
           w_up, conv_w, conv_b, w_down, final_g):
    depth = w_in.shape[0]
    x = x.astype(_F32)

    def row(a):
        return a.astype(_F32)[:, None, :]

    mixer_small = (row(norm1_g), row(b_gate), pool_w.astype(_BF16), row(pool_scale), row(hgrn_norm_g))
    ffn_small = (row(norm2_g), conv_w.astype(_F32), row(conv_b))
    mixer_f32 = tuple(w.astype(_F32) for w in (w_in, w_pa, w_pb, w_o))
    ffn_f32 = tuple(w.astype(_F32) for w in (w_up, w_down))
    lbl = lb_logits.astype(_F32)
    fg = final_g.astype(_F32)[None, :]
    mixer_w = tuple(w[0].astype(_BF16) for w in mixer_f32)
    for l in range(depth):
        x, *ffn_w = _mixer(x, mixer_small, mixer_w, lbl, layer=l, convert=ffn_f32)
        last = l == depth - 1
        x, *mixer_w = _ffn(x, ffn_small, ffn_w, fg, layer=l, final_norm=last,
                           convert=() if last else mixer_f32, convert_layer=l + 1)
    return x
```

```python
import functools

import numpy as np
import jax
import jax.numpy as jnp
from jax import lax
from jax.experimental import pallas as pl
from jax.experimental.pallas import tpu as pltpu

EPS = 1e-6
POOL_WINDOWS = (2, 4, 8, 16)
POOL_HISTORY = 16
HEAD_DIM = 128
HGRN_CHUNK = 128
CONV_WIDTH = 3
CONV_TAIL = 8
V7X_VMEM_LIMIT_BYTES = 56 * 1024 * 1024
PROJ_BLOCK = 256
MIXER_TILE = 512
FFN_TILE = 1024
LOG2E = 1.4426950408889634
ROW_PACK = 16
MATMUL_LEVELS = 3

_F32 = jnp.float32
_BF16 = jnp.bfloat16


def _sigmoid(x):
    return 0.5 * jnp.tanh(0.5 * x) + 0.5


def _silu(x):
    h = 0.5 * x
    return h * jnp.tanh(h) + h


def _rmsnorm(x, g):
    return x * lax.rsqrt(jnp.mean(x * x, axis=-1, keepdims=True) + EPS) * g


def _dot(a, b):
    return jnp.dot(a, b, preferred_element_type=_F32)


def _dot_nt(a, b):
    return lax.dot_general(a, b, (((1,), (1,)), ((), ())), preferred_element_type=_F32)


def _dot_tn(a, b):
    return lax.dot_general(a, b, (((0,), (0,)), ((), ())), preferred_element_type=_F32)


def _hierarchy_tables(chunk):
    n_levels = int(np.log2(chunk))
    assert 1 << n_levels == chunk
    t = np.arange(chunk)[:, None]
    u = np.arange(chunk)[None, :]
    mats = []
    for l in range(1, MATMUL_LEVELS):
        m = 1 << l
        p = t % (2 * m)
        upper = p >= m
        block_start = t - p + m
        block_end = t - p + m - 1
        mats.append(np.where(upper, (u >= block_start) & (u <= t), (u > t) & (u <= block_end)))
    mats.append(u <= t)
    dmat = np.concatenate(mats, axis=0).astype(np.float32)
    dmat = np.concatenate([dmat, dmat], axis=1)
    xor = t ^ u
    level = np.full((chunk, chunk), n_levels + 8, np.int32)
    lower = t > u
    level[lower] = np.floor(np.log2(xor[lower])).astype(np.int32)
    level[np.arange(chunk), np.arange(chunk)] = -1
    return dmat, level, n_levels


def _mixer_kernel(x_ref, n1g_ref, w_in_ref, bg_ref, pw_ref, ps_ref, lbl_ref, hng_ref, w_pa_ref, w_pb_ref,
                  w_o_ref, dmat_ref, lvl_ref, out_ref, qb_scr, qf_scr, kb_scr, vb_scr, gs_scr, so_scr, gate_scr,
                  w_scr, dec_scr, o_scr, uh_scr, st_scr, *, layer, n_levels):
    tm, d_model = x_ref.shape
    n_heads = st_scr.shape[0]
    width = n_heads * HEAD_DIM
    chunk = HGRN_CHUNK
    n_chunks = tm // chunk
    i = pl.program_id(1)

    @pl.when(i == 0)
    def _():
        uh_scr[...] = jnp.zeros_like(uh_scr)
        st_scr[...] = jnp.zeros_like(st_scr)

    x = x_ref[...]
    xn = _rmsnorm(x, n1g_ref[...]).astype(_BF16)

    def proj(c0, n):
        return _dot(xn, w_in_ref[:, c0:c0 + n])

    logits = lbl_ref[...]
    e = jnp.exp(logits - jnp.max(logits, axis=0, keepdims=True))
    soft = e / jnp.sum(e, axis=0, keepdims=True)
    lb = jnp.zeros((1, width), _F32)
    for r in range(1, layer + 1):
        lb = lb + soft[r:r + 1, :]

    for c0 in range(0, width, PROJ_BLOCK):
        cols = slice(c0, c0 + PROJ_BLOCK)
        q = _silu(proj(width + c0, PROJ_BLOCK))
        qb_scr[:, cols] = q.astype(_BF16)
        sig = jnp.exp(-jnp.log(1.0 + jnp.exp(-proj(2 * width + c0, PROJ_BLOCK))))
        f = lb[:, cols] + (1.0 - lb[:, cols]) * sig
        qf_scr[:, cols] = (q * f).astype(_BF16)
        kb_scr[:, cols] = (1.0 - f).astype(_BF16)
        g = jnp.log(f) * LOG2E
        g_hi = g.astype(_BF16)
        g_lo = (g - g_hi.astype(_F32)).astype(_BF16)
        for c in range(n_chunks):
            gs_scr[c, 0:chunk, cols] = g_hi[c * chunk:(c + 1) * chunk, :]
            gs_scr[c, chunk:2 * chunk, cols] = g_lo[c * chunk:(c + 1) * chunk, :]
        vb_scr[:, cols] = proj(3 * width + c0, PROJ_BLOCK).astype(_BF16)
        so_scr[:, cols] = _silu(proj(4 * width + c0, PROJ_BLOCK)).astype(_BF16)

    u = jnp.concatenate([proj(c0, PROJ_BLOCK) for c0 in range(0, width, PROJ_BLOCK)], axis=1)
    ext = jnp.concatenate([uh_scr[...], u], axis=0)
    uh_scr[...] = u[tm - POOL_HISTORY:, :]
    head_frames = (i * tm + 1 + lax.broadcasted_iota(jnp.int32, (POOL_HISTORY, HEAD_DIM), 0)).astype(_F32)
    gate_starts = list(range(0, 2 * d_model, PROJ_BLOCK))
    gates_per_group = -(-len(gate_starts) // len(POOL_WINDOWS))
    sums = ext
    mixed = []
    for gi, w in enumerate(POOL_WINDOWS):
        sums = sums[:, (HEAD_DIM if gi else 0):]
        sums = sums + pltpu.roll(sums, w // 2, 0)
        win = sums[POOL_HISTORY:, 0:HEAD_DIM]
        u_g = u[:, gi * HEAD_DIM:(gi + 1) * HEAD_DIM]
        inv_count = jnp.concatenate([1.0 / jnp.minimum(head_frames, float(w)),
                                     jnp.full((tm - POOL_HISTORY, HEAD_DIM), 1.0 / w, _F32)], axis=0)
        pooled = win * inv_count - u_g
        mixed.append(_dot(pooled.astype(_BF16), pw_ref[gi]))
        for c0 in gate_starts[gi * gates_per_group:(gi + 1) * gates_per_group]:
            cols = slice(c0, c0 + PROJ_BLOCK)
            gate_scr[:, cols] = _sigmoid(proj(5 * width + c0, PROJ_BLOCK) + bg_ref[:, cols]).astype(_BF16)
    mixed = jnp.concatenate(mixed, axis=1) * ps_ref[...]
    ya = _dot(mixed.astype(_BF16), w_pa_ref[...])

    level = lvl_ref[...]

    def tile_of(ref, c, h):
        return ref[c * chunk:(c + 1) * chunk, h * HEAD_DIM:(h + 1) * HEAD_DIM]

    def decay_weights(c):
        expo = _dot(dmat_ref[...], gs_scr[c])
        n_mm = (MATMUL_LEVELS - 1) * chunk
        b = expo[n_mm:, :]
        parts = [expo[:n_mm, :]]
        for l in range(MATMUL_LEVELS, n_levels):
            m = 1 << l
            for g0 in range(0, chunk, 2 * m):
                ref = jnp.broadcast_to(b[g0 + m - 1:g0 + m, :], (m, width))
                parts.append(ref - b[g0:g0 + m, :])
                parts.append(b[g0 + m:g0 + 2 * m, :] - ref)
        parts.append(b)
        parts.append(jnp.broadcast_to(b[chunk - 1:chunk, :], (chunk, width)) - b)
        wgt = jnp.exp2(jnp.concatenate(parts, axis=0)).astype(_BF16)
        w_scr[c] = wgt
        dec_scr[c:c + 1, :] = jnp.exp2(b[chunk - 1:chunk, :])
        return {(l, h): wgt[(l - 1) * chunk:l * chunk, h * HEAD_DIM:(h + 1) * HEAD_DIM].T
                for l in range(1, n_levels) for h in range(n_heads)}

    scores = {}

    def level_scores(c, w_tr):
        q_c = qb_scr[c * chunk:(c + 1) * chunk, :]
        for h in range(n_heads):
            lanes = slice(h * HEAD_DIM, (h + 1) * HEAD_DIM)
            k_t = tile_of(kb_scr, c, h)
            k_tr = k_t.T
            acc = jnp.where(level == -1, _dot_nt(tile_of(qb_scr, c, h), k_t), 0.0)
            acc = jnp.where(level == 0, _dot_nt(tile_of(qf_scr, c, h), k_t), acc)
            for l in range(1, n_levels):
                m = 1 << l
                wgt = w_scr[c, (l - 1) * chunk:l * chunk, lanes]
                rhs = k_tr * w_tr[l, h]
                if m < ROW_PACK:
                    acc = jnp.where(level == l, _dot(q_c[:, lanes] * wgt, rhs), acc)
                    continue
                upper = [slice(g0 + m, g0 + 2 * m) for g0 in range(0, chunk, 2 * m)]
                part = _dot(jnp.concatenate([q_c[r, lanes] * wgt[r, :] for r in upper], axis=0), rhs)
                pieces = []
                for gi, r in enumerate(upper):
                    pieces.append(acc[r.start - m:r.start, :])
                    pieces.append(jnp.where(level[r, :] == l, part[gi * m:(gi + 1) * m, :], acc[r, :]))
                acc = jnp.concatenate(pieces, axis=0)
            scores[c, h] = acc

    pending = None
    for c in range(n_chunks):
        w_tr = decay_weights(c)
        if pending is not None:
            level_scores(*pending)
        pending = (c, w_tr)
    level_scores(*pending)

    hng = hng_ref[...]
    for c in range(n_chunks):
        rows = slice(c * chunk, (c + 1) * chunk)
        q_in = qb_scr[rows, :] * w_scr[c, (n_levels - 1) * chunk:n_levels * chunk, :]
        k_out = kb_scr[rows, :] * w_scr[c, n_levels * chunk:(n_levels + 1) * chunk, :]
        for h in range(n_heads):
            lanes = slice(h * HEAD_DIM, (h + 1) * HEAD_DIM)
            st = st_scr[h]
            v_t = vb_scr[rows, lanes]
            o = _dot(scores[c, h].astype(_BF16), v_t) + _dot_nt(q_in[:, lanes], st.astype(_BF16))
            st_scr[h] = dec_scr[c:c + 1, lanes] * st + _dot_tn(v_t, k_out[:, lanes])
            o = o * lax.rsqrt(jnp.mean(o * o, axis=-1, keepdims=True) + EPS) * hng
            o_scr[rows, lanes] = (o * so_scr[rows, lanes]).astype(_BF16)
    yb = _dot(o_scr[...], w_pb_ref[...])

    merged = gate_scr[:, :d_model] * ya + gate_scr[:, d_model:] * yb
    out_ref[...] = x + _dot(merged.astype(_BF16), w_o_ref[...])


def _ffn_kernel(x_ref, n2g_ref, w_up_ref, cw_ref, cb_ref, w_dn_ref, fg_ref, out_ref, a_scr, tail_scr, *,
                block, final_norm):
    tm, d_model = x_ref.shape
    d_ff = w_dn_ref.shape[0]
    i = pl.program_id(1)

    @pl.when(i == 0)
    def _():
        tail_scr[...] = jnp.zeros_like(tail_scr)

    x = x_ref[...]
    xn = _rmsnorm(x, n2g_ref[...]).astype(_BF16)

    def conv_block(c0, scale):
        h = _dot(xn, w_up_ref[:, c0:c0 + block])
        ext = jnp.concatenate([tail_scr[:, c0:c0 + block], h], axis=0)
        tail_scr[:, c0:c0 + block] = h[tm - CONV_TAIL:, :]
        h1 = pltpu.roll(ext, 1, 0)[CONV_TAIL:, :]
        h2 = pltpu.roll(ext, 2, 0)[CONV_TAIL:, :]
        cw = cw_ref[:, c0:c0 + block] * scale
        return cw[2:3, :] * h + cw[1:2, :] * h1 + cw[0:1, :] * h2 + cb_ref[:, c0:c0 + block] * scale

    for j in range(d_ff // block):
        val = conv_block(j * block, 1.0)
        half_gate = conv_block(d_ff + j * block, 0.5)
        a_scr[:, j * block:(j + 1) * block] = ((half_gate * jnp.tanh(half_gate) + half_gate) * val).astype(_BF16)
    y = x + _dot(a_scr[...], w_dn_ref[...])
    if final_norm:
        y = _rmsnorm(y, fg_ref[...])
    out_ref[...] = y


def _resident(shape):
    zeros = (0,) * len(shape)
    return pl.BlockSpec(shape, lambda b, i: zeros, pipeline_mode=pl.Buffered(1))


def _layer_of(stacked, layer):
    tail = (0,) * (stacked.ndim - 1)
    return pl.BlockSpec((None,) + stacked.shape[1:], lambda b, i: (layer,) + tail, pipeline_mode=pl.Buffered(1))


def _token_tile(seq, largest):
    tm = largest
    while tm >= HGRN_CHUNK:
        if seq % tm == 0:
            return tm
        tm //= 2
    raise ValueError(f"sequence length {seq} must be a multiple of {HGRN_CHUNK}")


def _params():
    return pltpu.CompilerParams(dimension_semantics=("arbitrary", "arbitrary"),
                                vmem_limit_bytes=V7X_VMEM_LIMIT_BYTES)


def _cast_plan(stacked, layer, n_outer, n_inner):
    steps = n_outer * n_inner
    rows = stacked.shape[1]
    n_blocks = 1
    for cand in range(steps, 0, -1):
        if steps % cand == 0 and rows % cand == 0 and (rows // cand) % ROW_PACK == 0:
            n_blocks = cand
            break
    per_block = steps // n_blocks
    block_rows = rows // n_blocks
    src = pl.BlockSpec((None, block_rows, stacked.shape[2]), lambda b, i: (layer, (b * n_inner + i) // per_block, 0))
    dst = pl.BlockSpec((block_rows, stacked.shape[2]), lambda b, i: ((b * n_inner + i) // per_block, 0))
    return src, dst, jax.ShapeDtypeStruct(stacked.shape[1:], _BF16)


def _with_side_casts(body, n_in, n_cast):
    def kernel_fn(*refs):
        inputs = refs[:n_in]
        sources = refs[n_in:n_in + n_cast]
        out_ref = refs[n_in + n_cast]
        targets = refs[n_in + n_cast + 1:n_in + 2 * n_cast + 1]
        scratch = refs[n_in + 2 * n_cast + 1:]
        for src, dst in zip(sources, targets):
            dst[...] = src[...].astype(dst.dtype)
        body(*inputs, out_ref, *scratch)
    return kernel_fn


def _mixer(x, small, weights, lbl, *, layer, convert=()):
    batch, seq, d_model = x.shape
    tm = _token_tile(seq, MIXER_TILE)
    n1g, bg, pw, ps, hng = small
    w_in, w_pa, w_pb, w_o = weights
    width = w_pa.shape[0]
    n_heads = width // HEAD_DIM
    dmat, level, n_levels = _hierarchy_tables(HGRN_CHUNK)
    dmat, level = jnp.asarray(dmat, _BF16), jnp.asarray(level)
    n_chunks = tm // HGRN_CHUNK
    grid = (batch, seq // tm)
    tile = pl.BlockSpec((None, tm, d_model), lambda b, i: (b, i, 0))
    per_layer = functools.partial(_layer_of, layer=layer)
    plans = [_cast_plan(w, layer, *grid) for w in convert]
    inputs = (x, n1g, w_in, bg, pw, ps, lbl, hng, w_pa, w_pb, w_o, dmat, level)
    in_specs = [tile, per_layer(n1g), _resident(w_in.shape), per_layer(bg), per_layer(pw), per_layer(ps),
                _resident(lbl.shape), per_layer(hng), _resident(w_pa.shape), _resident(w_pb.shape),
                _resident(w_o.shape), _resident(dmat.shape), _resident(level.shape)]
    body = functools.partial(_mixer_kernel, layer=layer, n_levels=n_levels)
    return pl.pallas_call(
        _with_side_casts(body, len(inputs), len(convert)),
        out_shape=[jax.ShapeDtypeStruct(x.shape, _F32)] + [p[2] for p in plans],
        grid=grid,
        in_specs=in_specs + [p[0] for p in plans],
        out_specs=[tile] + [p[1] for p in plans],
        scratch_shapes=[
            pltpu.VMEM((tm, width), _BF16),
            pltpu.VMEM((tm, width), _BF16),
            pltpu.VMEM((tm, width), _BF16),
            pltpu.VMEM((tm, width), _BF16),
            pltpu.VMEM((n_chunks, 2 * HGRN_CHUNK, width), _BF16),
            pltpu.VMEM((tm, width), _BF16),
            pltpu.VMEM((tm, 2 * d_model), _BF16),
            pltpu.VMEM((n_chunks, (n_levels + 1) * HGRN_CHUNK, width), _BF16),
            pltpu.VMEM((8 * ((n_chunks + 7) // 8), width), _F32),
            pltpu.VMEM((tm, width), _BF16),
            pltpu.VMEM((POOL_HISTORY, width), _F32),
            pltpu.VMEM((n_heads, HEAD_DIM, HEAD_DIM), _F32),
        ],
        compiler_params=_params(),
        name=f"mixer_l{layer}",
    )(*inputs, *convert)


def _ffn(x, small, weights, fg, *, layer, final_norm, convert=(), convert_layer=0):
    batch, seq, d_model = x.shape
    tm = _token_tile(seq, FFN_TILE)
    n2g, cw, cb = small
    w_up, w_dn = weights
    d_ff = w_dn.shape[0]
    block = 256 if d_ff % 256 == 0 else 128
    grid = (batch, seq // tm)
    tile = pl.BlockSpec((None, tm, d_model), lambda b, i: (b, i, 0))
    per_layer = functools.partial(_layer_of, layer=layer)
    plans = [_cast_plan(w, convert_layer, *grid) for w in convert]
    inputs = (x, n2g, w_up, cw, cb, w_dn, fg)
    in_specs = [tile, per_layer(n2g), _resident(w_up.shape), per_layer(cw), per_layer(cb), _resident(w_dn.shape),
                _resident(fg.shape)]
    body = functools.partial(_ffn_kernel, block=block, final_norm=final_norm)
    return pl.pallas_call(
        _with_side_casts(body, len(inputs), len(convert)),
        out_shape=[jax.ShapeDtypeStruct(x.shape, _F32)] + [p[2] for p in plans],
        grid=grid,
        in_specs=in_specs + [p[0] for p in plans],
        out_specs=[tile] + [p[1] for p in plans],
        scratch_shapes=[
            pltpu.VMEM((tm, d_ff), _BF16),
            pltpu.VMEM((CONV_TAIL, 2 * d_ff), _F32),
        ],
        compiler_params=_params(),
        name=f"ffn_l{layer}",
    )(*inputs, *convert)


def kernel(x, norm1_g, w_in, b_gate, pool_w, pool_scale, lb_logits, hgrn_norm_g, w_pa, w_pb, w_o, norm2_g,
           w_up, conv_w, conv_b, w_down, final_g):
    depth = w_in.shape[0]
    x = x.astype(_F32)

    def row(a):
        return a.astype(_F32)[:, None, :]

    mixer_small = (row(norm1_g), row(b_gate), pool_w.astype(_BF16), row(pool_scale), row(hgrn_norm_g))
    ffn_small = (row(norm2_g), conv_w.astype(_F32), row(conv_b))
    mixer_f32 = tuple(w.astype(_F32) for w in (w_in, w_pa, w_pb, w_o))
    ffn_f32 = tuple(w.astype(_F32) for w in (w_up, w_down))
    lbl = lb_logits.astype(_F32)
    fg = final_g.astype(_F32)[None, :]
    mixer_w = tuple(w[0].astype(_BF16) for w in mixer_f32)
    for l in range(depth):
        x, *ffn_w = _mixer(x, mixer_small, mixer_w, lbl, layer=l, convert=ffn_f32)
        last = l == depth - 1
        x, *mixer_w = _ffn(x, ffn_small, ffn_w, fg, layer=l, final_norm=last,
                           convert=() if last else mixer_f32, convert_layer=l + 1)
    return x
```

```python
import functools

import numpy as np
import jax
import jax.numpy as jnp
from jax import lax
from jax.experimental import pallas as pl
from jax.experimental.pallas import tpu as pltpu

EPS = 1e-6
POOL_WINDOWS = (2, 4, 8, 16)
POOL_HISTORY = 16
HEAD_DIM = 128
HGRN_CHUNK = 128
CONV_WIDTH = 3
CONV_TAIL = 8
V7X_VMEM_LIMIT_BYTES = 56 * 1024 * 1024
PROJ_BLOCK = 256
MIXER_TILE = 512
FFN_TILE = 1024
LOG2E = 1.4426950408889634
ROW_PACK = 16
MATMUL_LEVELS = 3

_F32 = jnp.float32
_BF16 = jnp.bfloat16


def _sigmoid(x):
    return 0.5 * jnp.tanh(0.5 * x) + 0.5


def _silu(x):
    h = 0.5 * x
    return h * jnp.tanh(h) + h


def _rmsnorm(x, g):
    return x * lax.rsqrt(jnp.mean(x * x, axis=-1, keepdims=True) + EPS) * g


def _dot(a, b):
    return jnp.dot(a, b, preferred_element_type=_F32)


def _dot_nt(a, b):
    return lax.dot_general(a, b, (((1,), (1,)), ((), ())), preferred_element_type=_F32)


def _dot_tn(a, b):
    return lax.dot_general(a, b, (((0,), (0,)), ((), ())), preferred_element_type=_F32)


def _hierarchy_tables(chunk):
    n_levels = int(np.log2(chunk))
    assert 1 << n_levels == chunk
    t = np.arange(chunk)[:, None]
    u = np.arange(chunk)[None, :]
    mats = []
    for l in range(1, MATMUL_LEVELS):
        m = 1 << l
        p = t % (2 * m)
        upper = p >= m
        block_start = t - p + m
        block_end = t - p + m - 1
        mats.append(np.where(upper, (u >= block_start) & (u <= t), (u > t) & (u <= block_end)))
    mats.append(u <= t)
    dmat = np.concatenate(mats, axis=0).astype(np.float32)
    dmat = np.concatenate([dmat, dmat], axis=1)
    xor = t ^ u
    level = np.full((chunk, chunk), n_levels + 8, np.int32)
    lower = t > u
    level[lower] = np.floor(np.log2(xor[lower])).astype(np.int32)
    level[np.arange(chunk), np.arange(chunk)] = -1
    return dmat, level, n_levels


def _mixer_kernel(x_ref, n1g_ref, w_in_ref, bg_ref, pw_ref, ps_ref, lbl_ref, hng_ref, w_pa_ref, w_pb_ref,
                  w_o_ref, dmat_ref, lvl_ref, out_ref, qb_scr, qf_scr, kb_scr, vb_scr, gs_scr, so_scr, gate_scr,
                  w_scr, dec_scr, o_scr, uh_scr, st_scr, *, layer, n_levels):
    tm, d_model = x_ref.shape
    n_heads = st_scr.shape[0]
    width = n_heads * HEAD_DIM
    chunk = HGRN_CHUNK
    n_chunks = tm // chunk
    i = pl.program_id(1)

    @pl.when(i == 0)
    def _():
        uh_scr[...] = jnp.zeros_like(uh_scr)
        st_scr[...] = jnp.zeros_like(st_scr)

    x = x_ref[...]
    xn = _rmsnorm(x, n1g_ref[...]).astype(_BF16)

    def proj(c0, n):
        return _dot(xn, w_in_ref[:, c0:c0 + n])

    logits = lbl_ref[...]
    e = jnp.exp(logits - jnp.max(logits, axis=0, keepdims=True))
    soft = e / jnp.sum(e, axis=0, keepdims=True)
    lb = jnp.zeros((1, width), _F32)
    for r in range(1, layer + 1):
        lb = lb + soft[r:r + 1, :]

    for c0 in range(0, width, PROJ_BLOCK):
        cols = slice(c0, c0 + PROJ_BLOCK)
        q = _silu(proj(width + c0, PROJ_BLOCK))
        qb_scr[:, cols] = q.astype(_BF16)
        sig = jnp.exp(-jnp.log(1.0 + jnp.exp(-proj(2 * width + c0, PROJ_BLOCK))))
        f = lb[:, cols] + (1.0 - lb[:, cols]) * sig
        qf_scr[:, cols] = (q * f).astype(_BF16)
        kb_scr[:, cols] = (1.0 - f).astype(_BF16)
        g = jnp.log(f) * LOG2E
        g_hi = g.astype(_BF16)
        g_lo = (g - g_hi.astype(_F32)).astype(_BF16)
        for c in range(n_chunks):
            gs_scr[c, 0:chunk, cols] = g_hi[c * chunk:(c + 1) * chunk, :]
            gs_scr[c, chunk:2 * chunk, cols] = g_lo[c * chunk:(c + 1) * chunk, :]
        vb_scr[:, cols] = proj(3 * width + c0, PROJ_BLOCK).astype(_BF16)
        so_scr[:, cols] = _silu(proj(4 * width + c0, PROJ_BLOCK))

    u = jnp.concatenate([proj(c0, PROJ_BLOCK) for c0 in range(0, width, PROJ_BLOCK)], axis=1)
    ext = jnp.concatenate([uh_scr[...], u], axis=0)
    uh_scr[...] = u[tm - POOL_HISTORY:, :]
    head_frames = (i * tm + 1 + lax.broadcasted_iota(jnp.int32, (POOL_HISTORY, HEAD_DIM), 0)).astype(_F32)
    sums = ext
    mixed = []
    for gi, w in enumerate(POOL_WINDOWS):
        sums = sums[:, (HEAD_DIM if gi else 0):]
        sums = sums + pltpu.roll(sums, w // 2, 0)
        win = sums[POOL_HISTORY:, 0:HEAD_DIM]
        u_g = u[:, gi * HEAD_DIM:(gi + 1) * HEAD_DIM]
        inv_count = jnp.concatenate([1.0 / jnp.minimum(head_frames, float(w)),
                                     jnp.full((tm - POOL_HISTORY, HEAD_DIM), 1.0 / w, _F32)], axis=0)
        pooled = win * inv_count - u_g
        mixed.append(_dot(pooled.astype(_BF16), pw_ref[gi]))
    mixed = jnp.concatenate(mixed, axis=1) * ps_ref[...]
    ya = _dot(mixed.astype(_BF16), w_pa_ref[...])

    level = lvl_ref[...]

    def tile_of(ref, c, h):
        return ref[c * chunk:(c + 1) * chunk, h * HEAD_DIM:(h + 1) * HEAD_DIM]

    def decay_weights(c):
        expo = _dot(dmat_ref[...], gs_scr[c])
        n_mm = (MATMUL_LEVELS - 1) * chunk
        b = expo[n_mm:, :]
        parts = [expo[:n_mm, :]]
        for l in range(MATMUL_LEVELS, n_levels):
            m = 1 << l
            for g0 in range(0, chunk, 2 * m):
                ref = jnp.broadcast_to(b[g0 + m - 1:g0 + m, :], (m, width))
                parts.append(ref - b[g0:g0 + m, :])
                parts.append(b[g0 + m:g0 + 2 * m, :] - ref)
        parts.append(b)
        parts.append(jnp.broadcast_to(b[chunk - 1:chunk, :], (chunk, width)) - b)
        wgt = jnp.exp2(jnp.concatenate(parts, axis=0)).astype(_BF16)
        w_scr[c] = wgt
        dec_scr[c:c + 1, :] = jnp.exp2(b[chunk - 1:chunk, :])
        return {(l, h): wgt[(l - 1) * chunk:l * chunk, h * HEAD_DIM:(h + 1) * HEAD_DIM].T
                for l in range(1, n_levels) for h in range(n_heads)}

    scores = {}

    def level_scores(c, w_tr):
        k_tr = {}
        for h in range(n_heads):
            k_t = tile_of(kb_scr, c, h)
            k_tr[h] = k_t.T
            diag = jnp.where(level == -1, _dot_nt(tile_of(qb_scr, c, h), k_t), 0.0)
            scores[c, h] = jnp.where(level == 0, _dot_nt(tile_of(qf_scr, c, h), k_t), diag)
        for l in range(1, n_levels):
            m = 1 << l
            wgt = w_scr[c, (l - 1) * chunk:l * chunk, :]
            q_c = qb_scr[c * chunk:(c + 1) * chunk, :]
            if m < ROW_PACK:
                q_l = q_c * wgt
                for h in range(n_heads):
                    lanes = slice(h * HEAD_DIM, (h + 1) * HEAD_DIM)
                    scores[c, h] = jnp.where(level == l, _dot(q_l[:, lanes], k_tr[h] * w_tr[l, h]), scores[c, h])
                continue
            upper = [slice(g0 + m, g0 + 2 * m) for g0 in range(0, chunk, 2 * m)]
            q_l = jnp.concatenate([q_c[r, :] * wgt[r, :] for r in upper], axis=0)
            for h in range(n_heads):
                lanes = slice(h * HEAD_DIM, (h + 1) * HEAD_DIM)
                part = _dot(q_l[:, lanes], k_tr[h] * w_tr[l, h])
                old = scores[c, h]
                pieces = []
                for gi, r in enumerate(upper):
                    pieces.append(old[r.start - m:r.start, :])
                    pieces.append(jnp.where(level[r, :] == l, part[gi * m:(gi + 1) * m, :], old[r, :]))
                scores[c, h] = jnp.concatenate(pieces, axis=0)

    def gate_block(c0):
        cols = slice(c0, c0 + PROJ_BLOCK)
        gate_scr[:, cols] = _sigmoid(proj(5 * width + c0, PROJ_BLOCK) + bg_ref[:, cols])

    gate_starts = list(range(0, 2 * d_model, PROJ_BLOCK))
    slots = 2 * n_chunks
    per_slot = -(-len(gate_starts) // slots)

    def fill(slot):
        for c0 in gate_starts[slot * per_slot:(slot + 1) * per_slot]:
            gate_block(c0)

    pending = None
    for c in range(n_chunks):
        w_tr = decay_weights(c)
        fill(2 * c)
        if pending is not None:
            level_scores(*pending)
        fill(2 * c + 1)
        pending = (c, w_tr)
    level_scores(*pending)
    for c0 in gate_starts[slots * per_slot:]:
        gate_block(c0)

    hng = hng_ref[...]
    for c in range(n_chunks):
        rows = slice(c * chunk, (c + 1) * chunk)
        q_in = qb_scr[rows, :] * w_scr[c, (n_levels - 1) * chunk:n_levels * chunk, :]
        k_out = kb_scr[rows, :] * w_scr[c, n_levels * chunk:(n_levels + 1) * chunk, :]
        for h in range(n_heads):
            lanes = slice(h * HEAD_DIM, (h + 1) * HEAD_DIM)
            st = st_scr[h]
            v_t = vb_scr[rows, lanes]
            o = _dot(scores[c, h].astype(_BF16), v_t) + _dot_nt(q_in[:, lanes], st.astype(_BF16))
            st_scr[h] = dec_scr[c:c + 1, lanes] * st + _dot_tn(v_t, k_out[:, lanes])
            o = o * lax.rsqrt(jnp.mean(o * o, axis=-1, keepdims=True) + EPS) * hng
            o_scr[rows, lanes] = o * so_scr[rows, lanes]
    yb = _dot(o_scr[...].astype(_BF16), w_pb_ref[...])

    merged = gate_scr[:, :d_model] * ya + gate_scr[:, d_model:] * yb
    out_ref[...] = x + _dot(merged.astype(_BF16), w_o_ref[...])


def _ffn_kernel(x_ref, n2g_ref, w_up_ref, cw_ref, cb_ref, w_dn_ref, fg_ref, out_ref, a_scr, tail_scr, *,
                block, final_norm):
    tm, d_model = x_ref.shape
    d_ff = w_dn_ref.shape[0]
    i = pl.program_id(1)

    @pl.when(i == 0)
    def _():
        tail_scr[...] = jnp.zeros_like(tail_scr)

    x = x_ref[...]
    xn = _rmsnorm(x, n2g_ref[...]).astype(_BF16)

    def conv_block(c0, scale):
        h = _dot(xn, w_up_ref[:, c0:c0 + block])
        ext = jnp.concatenate([tail_scr[:, c0:c0 + block], h], axis=0)
        tail_scr[:, c0:c0 + block] = h[tm - CONV_TAIL:, :]
        h1 = pltpu.roll(ext, 1, 0)[CONV_TAIL:, :]
        h2 = pltpu.roll(ext, 2, 0)[CONV_TAIL:, :]
        cw = cw_ref[:, c0:c0 + block] * scale
        return cw[2:3, :] * h + cw[1:2, :] * h1 + cw[0:1, :] * h2 + cb_ref[:, c0:c0 + block] * scale

    for j in range(d_ff // block):
        val = conv_block(j * block, 1.0)
        half_gate = conv_block(d_ff + j * block, 0.5)
        a_scr[:, j * block:(j + 1) * block] = ((half_gate * jnp.tanh(half_gate) + half_gate) * val).astype(_BF16)
    y = x + _dot(a_scr[...], w_dn_ref[...])
    if final_norm:
        y = _rmsnorm(y, fg_ref[...])
    out_ref[...] = y


def _resident(shape):
    zeros = (0,) * len(shape)
    return pl.BlockSpec(shape, lambda b, i: zeros, pipeline_mode=pl.Buffered(1))


def _layer_of(stacked, layer):
    tail = (0,) * (stacked.ndim - 1)
    return pl.BlockSpec((None,) + stacked.shape[1:], lambda b, i: (layer,) + tail, pipeline_mode=pl.Buffered(1))


def _token_tile(seq, largest):
    tm = largest
    while tm >= HGRN_CHUNK:
        if seq % tm == 0:
            return tm
        tm //= 2
    raise ValueError(f"sequence length {seq} must be a multiple of {HGRN_CHUNK}")


def _params():
    return pltpu.CompilerParams(dimension_semantics=("arbitrary", "arbitrary"),
                                vmem_limit_bytes=V7X_VMEM_LIMIT_BYTES)


def _cast_plan(stacked, layer, n_outer, n_inner):
    steps = n_outer * n_inner
    rows = stacked.shape[1]
    n_blocks = 1
    for cand in range(steps, 0, -1):
        if steps % cand == 0 and rows % cand == 0 and (rows // cand) % ROW_PACK == 0:
            n_blocks = cand
            break
    per_block = steps // n_blocks
    block_rows = rows // n_blocks
    src = pl.BlockSpec((None, block_rows, stacked.shape[2]), lambda b, i: (layer, (b * n_inner + i) // per_block, 0))
    dst = pl.BlockSpec((block_rows, stacked.shape[2]), lambda b, i: ((b * n_inner + i) // per_block, 0))
    return src, dst, jax.ShapeDtypeStruct(stacked.shape[1:], _BF16)


def _with_side_casts(body, n_in, n_cast):
    def kernel_fn(*refs):
        inputs = refs[:n_in]
        sources = refs[n_in:n_in + n_cast]
        out_ref = refs[n_in + n_cast]
        targets = refs[n_in + n_cast + 1:n_in + 2 * n_cast + 1]
        scratch = refs[n_in + 2 * n_cast + 1:]
        for src, dst in zip(sources, targets):
            dst[...] = src[...].astype(dst.dtype)
        body(*inputs, out_ref, *scratch)
    return kernel_fn


def _mixer(x, small, weights, lbl, *, layer, convert=()):
    batch, seq, d_model = x.shape
    tm = _token_tile(seq, MIXER_TILE)
    n1g, bg, pw, ps, hng = small
    w_in, w_pa, w_pb, w_o = weights
    width = w_pa.shape[0]
    n_heads = width // HEAD_DIM
    dmat, level, n_levels = _hierarchy_tables(HGRN_CHUNK)
    dmat, level = jnp.asarray(dmat, _BF16), jnp.asarray(level)
    n_chunks = tm // HGRN_CHUNK
    grid = (batch, seq // tm)
    tile = pl.BlockSpec((None, tm, d_model), lambda b, i: (b, i, 0))
    per_layer = functools.partial(_layer_of, layer=layer)
    plans = [_cast_plan(w, layer, *grid) for w in convert]
    inputs = (x, n1g, w_in, bg, pw, ps, lbl, hng, w_pa, w_pb, w_o, dmat, level)
    in_specs = [tile, per_layer(n1g), _resident(w_in.shape), per_layer(bg), per_layer(pw), per_layer(ps),
                _resident(lbl.shape), per_layer(hng), _resident(w_pa.shape), _resident(w_pb.shape),
                _resident(w_o.shape), _resident(dmat.shape), _resident(level.shape)]
    body = functools.partial(_mixer_kernel, layer=layer, n_levels=n_levels)
    return pl.pallas_call(
        _with_side_casts(body, len(inputs), len(convert)),
        out_shape=[jax.ShapeDtypeStruct(x.shape, _F32)] + [p[2] for p in plans],
        grid=grid,
        in_specs=in_specs + [p[0] for p in plans],
        out_specs=[tile] + [p[1] for p in plans],
        scratch_shapes=[
            pltpu.VMEM((tm, width), _BF16),
            pltpu.VMEM((tm, width), _BF16),
            pltpu.VMEM((tm, width), _BF16),
            pltpu.VMEM((tm, width), _BF16),
            pltpu.VMEM((n_chunks, 2 * HGRN_CHUNK, width), _BF16),
            pltpu.VMEM((tm, width), _F32),
            pltpu.VMEM((tm, 2 * d_model), _F32),
            pltpu.VMEM((n_chunks, (n_levels + 1) * HGRN_CHUNK, width), _BF16),
            pltpu.VMEM((8 * ((n_chunks + 7) // 8), width), _F32),
            pltpu.VMEM((tm, width), _F32),
            pltpu.VMEM((POOL_HISTORY, width), _F32),
            pltpu.VMEM((n_heads, HEAD_DIM, HEAD_DIM), _F32),
        ],
        compiler_params=_params(),
        name=f"mixer_l{layer}",
    )(*inputs, *convert)


def _ffn(x, small, weights, fg, *, layer, final_norm, convert=(), convert_layer=0):
    batch, seq, d_model = x.shape
    tm = _token_tile(seq, FFN_TILE)
    n2g, cw, cb = small
    w_up, w_dn = weights
    d_ff = w_dn.shape[0]
    block = 256 if d_ff % 256 == 0 else 128
    grid = (batch, seq // tm)
    tile = pl.BlockSpec((None, tm, d_model), lambda b, i: (b, i, 0))
    per_layer = functools.partial(_layer_of, layer=layer)
    plans = [_cast_plan(w, convert_layer, *grid) for w in convert]
    inputs = (x, n2g, w_up, cw, cb, w_dn, fg)
    in_specs = [tile, per_layer(n2g), _resident(w_up.shape), per_layer(cw), per_layer(cb), _resident(w_dn.shape),
                _resident(fg.shape)]
    body = functools.partial(_ffn_kernel, block=block, final_norm=final_norm)
    return pl.pallas_call(
        _with_side_casts(body, len(inputs), len(convert)),
        out_shape=[jax.ShapeDtypeStruct(x.shape, _F32)] + [p[2] for p in plans],
        grid=grid,
        in_specs=in_specs + [p[0] for p in plans],
        out_specs=[tile] + [p[1] for p in plans],
        scratch_shapes=[
            pltpu.VMEM((tm, d_ff), _BF16),
            pltpu.VMEM((CONV_TAIL, 2 * d_ff), _F32),
        ],
        compiler_params=_params(),
        name=f"ffn_l{layer}",
    )(*inputs, *convert)


def kernel(x, norm1_g, w_in, b_gate, pool_w, pool_scale, lb_logits, hgrn_norm_g, w_pa, w_pb, w_o, norm2_g,
           w_up, conv_w, conv_b, w_down, final_g):
    depth = w_in.shape[0]
    x = x.astype(_F32)

    def row(a):
        return a.astype(_F32)[:, None, :]

    mixer_small = (row(norm1_g), row(b_gate), pool_w.astype(_BF16), row(pool_scale), row(hgrn_norm_g))
    ffn_small = (row(norm2_g), conv_w.astype(_F32), row(conv_b))
    mixer_f32 = tuple(w.astype(_F32) for w in (w_in, w_pa, w_pb, w_o))
    ffn_f32 = tuple(w.astype(_F32) for w in (w_up, w_down))
    lbl = lb_logits.astype(_F32)
    fg = final_g.astype(_F32)[None, :]
    mixer_w = tuple(w[0].astype(_BF16) for w in mixer_f32)
    for l in range(depth):
        x, *ffn_w = _mixer(x, mixer_small, mixer_w, lbl, layer=l, convert=ffn_f32)
        last = l == depth - 1
        x, *mixer_w = _ffn(x, ffn_small, ffn_w, fg, layer=l, final_norm=last,
                           convert=() if last else mixer_f32, convert_layer=l + 1)
    return x
```

```python
import functools

import numpy as np
import jax
import jax.numpy as jnp
from jax import lax
from jax.experimental import pallas as pl
from jax.experimental.pallas import tpu as pltpu

EPS = 1e-6
POOL_WINDOWS = (2, 4, 8, 16)
POOL_HISTORY = 16
HEAD_DIM = 128
HGRN_CHUNK = 128
CONV_WIDTH = 3
CONV_TAIL = 8
V7X_VMEM_LIMIT_BYTES = 56 * 1024 * 1024
PROJ_BLOCK = 256
MIXER_TILE = 512
FFN_TILE = 1024
LOG2E = 1.4426950408889634
ROW_PACK = 16
SUBLANES = 8
MATMUL_LEVELS = 2
assert 2 << MATMUL_LEVELS >= SUBLANES

_F32 = jnp.float32
_BF16 = jnp.bfloat16


def _sigmoid(x):
    return 0.5 * jnp.tanh(0.5 * x) + 0.5


def _silu(x):
    h = 0.5 * x
    return h * jnp.tanh(h) + h


def _rmsnorm(x, g):
    return x * lax.rsqrt(jnp.mean(x * x, axis=-1, keepdims=True) + EPS) * g


def _dot(a, b):
    return jnp.dot(a, b, preferred_element_type=_F32)


def _dot_nt(a, b):
    return lax.dot_general(a, b, (((1,), (1,)), ((), ())), preferred_element_type=_F32)


def _dot_tn(a, b):
    return lax.dot_general(a, b, (((0,), (0,)), ((), ())), preferred_element_type=_F32)


def _hierarchy_tables(chunk):
    n_levels = int(np.log2(chunk))
    assert 1 << n_levels == chunk
    t = np.arange(chunk)[:, None]
    u = np.arange(chunk)[None, :]
    mats = []
    for l in range(1, MATMUL_LEVELS):
        m = 1 << l
        p = t % (2 * m)
        upper = p >= m
        block_start = t - p + m
        block_end = t - p + m - 1
        mats.append(np.where(upper, (u >= block_start) & (u <= t), (u > t) & (u <= block_end)))
    mats.append(u <= t)
    dmat = np.concatenate(mats, axis=0).astype(np.float32)
    dmat = np.concatenate([dmat, dmat], axis=1)
    xor = t ^ u
    level = np.full((chunk, chunk), n_levels + 8, np.int32)
    lower = t > u
    level[lower] = np.floor(np.log2(xor[lower])).astype(np.int32)
    level[np.arange(chunk), np.arange(chunk)] = -1
    return dmat, level, n_levels


def _mixer_kernel(x_ref, n1g_ref, w_in_ref, bg_ref, pw_ref, ps_ref, lbl_ref, hng_ref, w_pa_ref, w_pb_ref,
                  w_o_ref, dmat_ref, lvl_ref, out_ref, qb_scr, qf_scr, kb_scr, vb_scr, gs_scr, so_scr, gate_scr,
                  w_scr, dec_scr, o_scr, uh_scr, st_scr, *, layer, n_levels):
    tm, d_model = x_ref.shape
    n_heads = st_scr.shape[0]
    width = n_heads * HEAD_DIM
    chunk = HGRN_CHUNK
    n_chunks = tm // chunk
    i = pl.program_id(1)

    @pl.when(i == 0)
    def _():
        uh_scr[...] = jnp.zeros_like(uh_scr)
        st_scr[...] = jnp.zeros_like(st_scr)

    x = x_ref[...]
    xn = _rmsnorm(x, n1g_ref[...]).astype(_BF16)

    def proj(c0, n):
        return _dot(xn, w_in_ref[:, c0:c0 + n])

    logits = lbl_ref[...]
    e = jnp.exp(logits - jnp.max(logits, axis=0, keepdims=True))
    soft = e / jnp.sum(e, axis=0, keepdims=True)
    lb = jnp.zeros((1, width), _F32)
    for r in range(1, layer + 1):
        lb = lb + soft[r:r + 1, :]

    for c0 in range(0, width, PROJ_BLOCK):
        cols = slice(c0, c0 + PROJ_BLOCK)
        q = _silu(proj(width + c0, PROJ_BLOCK))
        qb_scr[:, cols] = q.astype(_BF16)
        sig = jnp.exp(-jnp.log(1.0 + jnp.exp(-proj(2 * width + c0, PROJ_BLOCK))))
        f = lb[:, cols] + (1.0 - lb[:, cols]) * sig
        qf_scr[:, cols] = (q * f).astype(_BF16)
        kb_scr[:, cols] = (1.0 - f).astype(_BF16)
        g = jnp.log(f) * LOG2E
        g_hi = g.astype(_BF16)
        g_lo = (g - g_hi.astype(_F32)).astype(_BF16)
        for c in range(n_chunks):
            gs_scr[c, 0:chunk, cols] = g_hi[c * chunk:(c + 1) * chunk, :]
            gs_scr[c, chunk:2 * chunk, cols] = g_lo[c * chunk:(c + 1) * chunk, :]
        vb_scr[:, cols] = proj(3 * width + c0, PROJ_BLOCK).astype(_BF16)
        so_scr[:, cols] = _silu(proj(4 * width + c0, PROJ_BLOCK))

    u = jnp.concatenate([proj(c0, PROJ_BLOCK) for c0 in range(0, width, PROJ_BLOCK)], axis=1)
    ext = jnp.concatenate([uh_scr[...], u], axis=0)
    uh_scr[...] = u[tm - POOL_HISTORY:, :]
    head_frames = (i * tm + 1 + lax.broadcasted_iota(jnp.int32, (POOL_HISTORY, HEAD_DIM), 0)).astype(_F32)
    sums = ext
    mixed = []
    for gi, w in enumerate(POOL_WINDOWS):
        sums = sums[:, (HEAD_DIM if gi else 0):]
        sums = sums + pltpu.roll(sums, w // 2, 0)
        win = sums[POOL_HISTORY:, 0:HEAD_DIM]
        u_g = u[:, gi * HEAD_DIM:(gi + 1) * HEAD_DIM]
        inv_count = jnp.concatenate([1.0 / jnp.minimum(head_frames, float(w)),
                                     jnp.full((tm - POOL_HISTORY, HEAD_DIM), 1.0 / w, _F32)], axis=0)
        pooled = win * inv_count - u_g
        mixed.append(_dot(pooled.astype(_BF16), pw_ref[gi]))
    mixed = jnp.concatenate(mixed, axis=1) * ps_ref[...]
    ya = _dot(mixed.astype(_BF16), w_pa_ref[...])

    level = lvl_ref[...]

    def tile_of(ref, c, h):
        return ref[c * chunk:(c + 1) * chunk, h * HEAD_DIM:(h + 1) * HEAD_DIM]

    def decay_weights(c):
        expo = _dot(dmat_ref[...], gs_scr[c])
        n_mm = (MATMUL_LEVELS - 1) * chunk
        b = expo[n_mm:, :]
        parts = [expo[:n_mm, :]]
        row = lax.broadcasted_iota(jnp.int32, (chunk, width), 0)
        for l in range(MATMUL_LEVELS, n_levels):
            m = 1 << l
            if 2 * m == SUBLANES:
                ref = jnp.concatenate([jnp.broadcast_to(b[t0 + m - 1:t0 + m, :], (SUBLANES, width))
                                       for t0 in range(0, chunk, SUBLANES)], axis=0)
                parts.append(jnp.where((row & (SUBLANES - 1)) < m, ref - b, b - ref))
                continue
            for g0 in range(0, chunk, 2 * m):
                ref = jnp.broadcast_to(b[g0 + m - 1:g0 + m, :], (m, width))
                parts.append(ref - b[g0:g0 + m, :])
                parts.append(b[g0 + m:g0 + 2 * m, :] - ref)
        parts.append(b)
        parts.append(jnp.broadcast_to(b[chunk - 1:chunk, :], (chunk, width)) - b)
        wgt = jnp.exp2(jnp.concatenate(parts, axis=0)).astype(_BF16)
        w_scr[c] = wgt
        dec_scr[c:c + 1, :] = jnp.exp2(b[chunk - 1:chunk, :])
        return {(l, h): wgt[(l - 1) * chunk:l * chunk, h * HEAD_DIM:(h + 1) * HEAD_DIM].T
                for l in range(1, n_levels) for h in range(n_heads)}

    scores = {}

    def level_scores(c, w_tr):
        k_tr = {}
        for h in range(n_heads):
            k_t = tile_of(kb_scr, c, h)
            k_tr[h] = k_t.T
            both = _dot_nt(jnp.concatenate([tile_of(qb_scr, c, h), tile_of(qf_scr, c, h)], axis=0), k_t)
            scores[c, h] = jnp.where(level == 0, both[chunk:, :], jnp.where(level == -1, both[:chunk, :], 0.0))
        for l in range(1, n_levels):
            m = 1 << l
            wgt = w_scr[c, (l - 1) * chunk:l * chunk, :]
            q_c = qb_scr[c * chunk:(c + 1) * chunk, :]
            if m < ROW_PACK:
                q_l = q_c * wgt
                for h in range(n_heads):
                    lanes = slice(h * HEAD_DIM, (h + 1) * HEAD_DIM)
                    scores[c, h] = jnp.where(level == l, _dot(q_l[:, lanes], k_tr[h] * w_tr[l, h]), scores[c, h])
                continue
            upper = [slice(g0 + m, g0 + 2 * m) for g0 in range(0, chunk, 2 * m)]
            q_l = jnp.concatenate([q_c[r, :] * wgt[r, :] for r in upper], axis=0)
            for h in range(n_heads):
                lanes = slice(h * HEAD_DIM, (h + 1) * HEAD_DIM)
                part = _dot(q_l[:, lanes], k_tr[h] * w_tr[l, h])
                old = scores[c, h]
                pieces = []
                for gi, r in enumerate(upper):
                    pieces.append(old[r.start - m:r.start, :])
                    pieces.append(jnp.where(level[r, :] == l, part[gi * m:(gi + 1) * m, :], old[r, :]))
                scores[c, h] = jnp.concatenate(pieces, axis=0)

    def gate_block(c0):
        cols = slice(c0, c0 + PROJ_BLOCK)
        gate_scr[:, cols] = _sigmoid(proj(5 * width + c0, PROJ_BLOCK) + bg_ref[:, cols])

    gate_starts = list(range(0, 2 * d_model, PROJ_BLOCK))
    slots = 2 * n_chunks
    per_slot = -(-len(gate_starts) // slots)

    def fill(slot):
        for c0 in gate_starts[slot * per_slot:(slot + 1) * per_slot]:
            gate_block(c0)

    pending = None
    for c in range(n_chunks):
        w_tr = decay_weights(c)
        fill(2 * c)
        if pending is not None:
            level_scores(*pending)
        fill(2 * c + 1)
        pending = (c, w_tr)
    level_scores(*pending)
    for c0 in gate_starts[slots * per_slot:]:
        gate_block(c0)

    hng = hng_ref[...]
    for c in range(n_chunks):
        rows = slice(c * chunk, (c + 1) * chunk)
        q_in = qb_scr[rows, :] * w_scr[c, (n_levels - 1) * chunk:n_levels * chunk, :]
        k_out = kb_scr[rows, :] * w_scr[c, n_levels * chunk:(n_levels + 1) * chunk, :]
        for h in range(n_heads):
            lanes = slice(h * HEAD_DIM, (h + 1) * HEAD_DIM)
            st = st_scr[h]
            v_t = vb_scr[rows, lanes]
            o = _dot(scores[c, h].astype(_BF16), v_t) + _dot(q_in[:, lanes], st.astype(_BF16))
            st_scr[h] = dec_scr[c:c + 1, lanes].T * st + _dot_tn(k_out[:, lanes], v_t)
            o = o * lax.rsqrt(jnp.mean(o * o, axis=-1, keepdims=True) + EPS) * hng
            o_scr[rows, lanes] = o * so_scr[rows, lanes]
    yb = _dot(o_scr[...].astype(_BF16), w_pb_ref[...])

    merged = gate_scr[:, :d_model] * ya + gate_scr[:, d_model:] * yb
    out_ref[...] = x + _dot(merged.astype(_BF16), w_o_ref[...])


def _ffn_kernel(x_ref, n2g_ref, w_up_ref, cw_ref, cb_ref, w_dn_ref, fg_ref, out_ref, a_scr, tail_scr, *,
                block, final_norm):
    tm, d_model = x_ref.shape
    d_ff = w_dn_ref.shape[0]
    i = pl.program_id(1)

    @pl.when(i == 0)
    def _():
        tail_scr[...] = jnp.zeros_like(tail_scr)

    x = x_ref[...]
    xn = _rmsnorm(x, n2g_ref[...]).astype(_BF16)

    def conv_block(c0, scale):
        h = _dot(xn, w_up_ref[:, c0:c0 + block])
        ext = jnp.concatenate([tail_scr[:, c0:c0 + block], h], axis=0)
        tail_scr[:, c0:c0 + block] = h[tm - CONV_TAIL:, :]
        h1 = pltpu.roll(ext, 1, 0)[CONV_TAIL:, :]
        h2 = pltpu.roll(ext, 2, 0)[CONV_TAIL:, :]
        cw = cw_ref[:, c0:c0 + block] * scale
        return cw[2:3, :] * h + cw[1:2, :] * h1 + cw[0:1, :] * h2 + cb_ref[:, c0:c0 + block] * scale

    for j in range(d_ff // block):
        val = conv_block(j * block, 1.0)
        half_gate = conv_block(d_ff + j * block, 0.5)
        a_scr[:, j * block:(j + 1) * block] = ((half_gate * jnp.tanh(half_gate) + half_gate) * val).astype(_BF16)
    y = x + _dot(a_scr[...], w_dn_ref[...])
    if final_norm:
        y = _rmsnorm(y, fg_ref[...])
    out_ref[...] = y


def _resident(shape):
    zeros = (0,) * len(shape)
    return pl.BlockSpec(shape, lambda b, i: zeros, pipeline_mode=pl.Buffered(1))


def _layer_of(stacked, layer):
    tail = (0,) * (stacked.ndim - 1)
    return pl.BlockSpec((None,) + stacked.shape[1:], lambda b, i: (layer,) + tail, pipeline_mode=pl.Buffered(1))


def _token_tile(seq, largest):
    tm = largest
    while tm >= HGRN_CHUNK:
        if seq % tm == 0:
            return tm
        tm //= 2
    raise ValueError(f"sequence length {seq} must be a multiple of {HGRN_CHUNK}")


def _params():
    return pltpu.CompilerParams(dimension_semantics=("arbitrary", "arbitrary"),
                                vmem_limit_bytes=V7X_VMEM_LIMIT_BYTES)


def _cast_plan(stacked, layer, n_outer, n_inner):
    steps = n_outer * n_inner
    rows = stacked.shape[1]
    n_blocks = 1
    for cand in range(steps, 0, -1):
        if steps % cand == 0 and rows % cand == 0 and (rows // cand) % ROW_PACK == 0:
            n_blocks = cand
            break
    per_block = steps // n_blocks
    block_rows = rows // n_blocks
    src = pl.BlockSpec((None, block_rows, stacked.shape[2]), lambda b, i: (layer, (b * n_inner + i) // per_block, 0))
    dst = pl.BlockSpec((block_rows, stacked.shape[2]), lambda b, i: ((b * n_inner + i) // per_block, 0))
    return src, dst, jax.ShapeDtypeStruct(stacked.shape[1:], _BF16)


def _with_side_casts(body, n_in, n_cast):
    def kernel_fn(*refs):
        inputs = refs[:n_in]
        sources = refs[n_in:n_in + n_cast]
        out_ref = refs[n_in + n_cast]
        targets = refs[n_in + n_cast + 1:n_in + 2 * n_cast + 1]
        scratch = refs[n_in + 2 * n_cast + 1:]
        for src, dst in zip(sources, targets):
            dst[...] = src[...].astype(dst.dtype)
        body(*inputs, out_ref, *scratch)
    return kernel_fn


def _mixer(x, small, weights, lbl, *, layer, convert=()):
    batch, seq, d_model = x.shape
    tm = _token_tile(seq, MIXER_TILE)
    n1g, bg, pw, ps, hng = small
    w_in, w_pa, w_pb, w_o = weights
    width = w_pa.shape[0]
    n_heads = width // HEAD_DIM
    dmat, level, n_levels = _hierarchy_tables(HGRN_CHUNK)
    dmat, level = jnp.asarray(dmat, _BF16), jnp.asarray(level)
    n_chunks = tm // HGRN_CHUNK
    grid = (batch, seq // tm)
    tile = pl.BlockSpec((None, tm, d_model), lambda b, i: (b, i, 0))
    per_layer = functools.partial(_layer_of, layer=layer)
    plans = [_cast_plan(w, layer, *grid) for w in convert]
    inputs = (x, n1g, w_in, bg, pw, ps, lbl, hng, w_pa, w_pb, w_o, dmat, level)
    in_specs = [tile, per_layer(n1g), _resident(w_in.shape), per_layer(bg), per_layer(pw), per_layer(ps),
                _resident(lbl.shape), per_layer(hng), _resident(w_pa.shape), _resident(w_pb.shape),
                _resident(w_o.shape), _resident(dmat.shape), _resident(level.shape)]
    body = functools.partial(_mixer_kernel, layer=layer, n_levels=n_levels)
    return pl.pallas_call(
        _with_side_casts(body, len(inputs), len(convert)),
        out_shape=[jax.ShapeDtypeStruct(x.shape, _F32)] + [p[2] for p in plans],
        grid=grid,
        in_specs=in_specs + [p[0] for p in plans],
        out_specs=[tile] + [p[1] for p in plans],
        scratch_shapes=[
            pltpu.VMEM((tm, width), _BF16),
            pltpu.VMEM((tm, width), _BF16),
            pltpu.VMEM((tm, width), _BF16),
            pltpu.VMEM((tm, width), _BF16),
            pltpu.VMEM((n_chunks, 2 * HGRN_CHUNK, width), _BF16),
            pltpu.VMEM((tm, width), _F32),
            pltpu.VMEM((tm, 2 * d_model), _F32),
            pltpu.VMEM((n_chunks, (n_levels + 1) * HGRN_CHUNK, width), _BF16),
            pltpu.VMEM((8 * ((n_chunks + 7) // 8), width), _F32),
            pltpu.VMEM((tm, width), _F32),
            pltpu.VMEM((POOL_HISTORY, width), _F32),
            pltpu.VMEM((n_heads, HEAD_DIM, HEAD_DIM), _F32),
        ],
        compiler_params=_params(),
        name=f"mixer_l{layer}",
    )(*inputs, *convert)


def _ffn(x, small, weights, fg, *, layer, final_norm, convert=(), convert_layer=0):
    batch, seq, d_model = x.shape
    tm = _token_tile(seq, FFN_TILE)
    n2g, cw, cb = small
    w_up, w_dn = weights
    d_ff = w_dn.shape[0]
    block = 256 if d_ff % 256 == 0 else 128
    grid = (batch, seq // tm)
    tile = pl.BlockSpec((None, tm, d_model), lambda b, i: (b, i, 0))
    per_layer = functools.partial(_layer_of, layer=layer)
    plans = [_cast_plan(w, convert_layer, *grid) for w in convert]
    inputs = (x, n2g, w_up, cw, cb, w_dn, fg)
    in_specs = [tile, per_layer(n2g), _resident(w_up.shape), per_layer(cw), per_layer(cb), _resident(w_dn.shape),
                _resident(fg.shape)]
    body = functools.partial(_ffn_kernel, block=block, final_norm=final_norm)
    return pl.pallas_call(
        _with_side_casts(body, len(inputs), len(convert)),
        out_shape=[jax.ShapeDtypeStruct(x.shape, _F32)] + [p[2] for p in plans],
        grid=grid,
        in_specs=in_specs + [p[0] for p in plans],
        out_specs=[tile] + [p[1] for p in plans],
        scratch_shapes=[
            pltpu.VMEM((tm, d_ff), _BF16),
            pltpu.VMEM((CONV_TAIL, 2 * d_ff), _F32),
        ],
        compiler_params=_params(),
        name=f"ffn_l{layer}",
    )(*inputs, *convert)


def kernel(x, norm1_g, w_in, b_gate, pool_w, pool_scale, lb_logits, hgrn_norm_g, w_pa, w_pb, w_o, norm2_g,
           w_up, conv_w, conv_b, w_down, final_g):
    depth = w_in.shape[0]
    x = x.astype(_F32)

    def row(a):
        return a.astype(_F32)[:, None, :]

    mixer_small = (row(norm1_g), row(b_gate), pool_w.astype(_BF16), row(pool_scale), row(hgrn_norm_g))
    ffn_small = (row(norm2_g), conv_w.astype(_F32), row(conv_b))
    mixer_f32 = tuple(w.astype(_F32) for w in (w_in, w_pa, w_pb, w_o))
    ffn_f32 = tuple(w.astype(_F32) for w in (w_up, w_down))
    lbl = lb_logits.astype(_F32)
    fg = final_g.astype(_F32)[None, :]
    mixer_w = tuple(w[0].astype(_BF16) for w in mixer_f32)
    for l in range(depth):
        x, *ffn_w = _mixer(x, mixer_small, mixer_w, lbl, layer=l, convert=ffn_f32)
        last = l == depth - 1
        x, *mixer_w = _ffn(x, ffn_small, ffn_w, fg, layer=l, final_norm=last,
                           convert=() if last else mixer_f32, convert_layer=l + 1)
    return x
```

```python
import functools

import numpy as np
import jax
import jax.numpy as jnp
from jax import lax
from jax.experimental import pallas as pl
from jax.experimental.pallas import tpu as pltpu

EPS = 1e-6
POOL_WINDOWS = (2, 4, 8, 16)
POOL_HISTORY = 16
HEAD_DIM = 128
HGRN_CHUNK = 128
CONV_WIDTH = 3
CONV_TAIL = 8
V7X_VMEM_LIMIT_BYTES = 56 * 1024 * 1024
PROJ_BLOCK = 256
MIXER_TILE = 512
FFN_TILE = 1024
LOG2E = 1.4426950408889634
ROW_PACK = 16
SUBLANES = 8
MATMUL_LEVELS = 2
assert 2 << MATMUL_LEVELS >= SUBLANES

_F32 = jnp.float32
_BF16 = jnp.bfloat16


def _sigmoid(x):
    return 0.5 * jnp.tanh(0.5 * x) + 0.5


def _silu(x):
    h = 0.5 * x
    return h * jnp.tanh(h) + h


def _rmsnorm(x, g):
    return x * lax.rsqrt(jnp.mean(x * x, axis=-1, keepdims=True) + EPS) * g


def _dot(a, b):
    return jnp.dot(a, b, preferred_element_type=_F32)


def _dot_nt(a, b):
    return lax.dot_general(a, b, (((1,), (1,)), ((), ())), preferred_element_type=_F32)


def _dot_tn(a, b):
    return lax.dot_general(a, b, (((0,), (0,)), ((), ())), preferred_element_type=_F32)


def _hierarchy_tables(chunk):
    n_levels = int(np.log2(chunk))
    assert 1 << n_levels == chunk
    t = np.arange(chunk)[:, None]
    u = np.arange(chunk)[None, :]
    mats = []
    for l in range(1, MATMUL_LEVELS):
        m = 1 << l
        p = t % (2 * m)
        upper = p >= m
        block_start = t - p + m
        block_end = t - p + m - 1
        mats.append(np.where(upper, (u >= block_start) & (u <= t), (u > t) & (u <= block_end)))
    mats.append(u <= t)
    dmat = np.concatenate(mats, axis=0).astype(np.float32)
    dmat = np.concatenate([dmat, dmat], axis=1)
    xor = t ^ u
    level = np.full((chunk, chunk), n_levels + 8, np.int32)
    lower = t > u
    level[lower] = np.floor(np.log2(xor[lower])).astype(np.int32)
    level[np.arange(chunk), np.arange(chunk)] = -1
    return dmat, level, n_levels


def _mixer_kernel(x_ref, n1g_ref, w_in_ref, bg_ref, pw_ref, ps_ref, lbl_ref, hng_ref, w_pa_ref, w_pb_ref,
                  w_o_ref, dmat_ref, lvl_ref, out_ref, qb_scr, qf_scr, kb_scr, vb_scr, gs_scr, so_scr, gate_scr,
                  w_scr, dec_scr, o_scr, kt_scr, uh_scr, st_scr, *, layer, n_levels):
    tm, d_model = x_ref.shape
    n_heads = st_scr.shape[0]
    width = n_heads * HEAD_DIM
    chunk = HGRN_CHUNK
    n_chunks = tm // chunk
    i = pl.program_id(1)

    @pl.when(i == 0)
    def _():
        uh_scr[...] = jnp.zeros_like(uh_scr)
        st_scr[...] = jnp.zeros_like(st_scr)

    x = x_ref[...]
    xn = _rmsnorm(x, n1g_ref[...]).astype(_BF16)

    def proj(c0, n):
        return _dot(xn, w_in_ref[:, c0:c0 + n])

    logits = lbl_ref[...]
    e = jnp.exp(logits - jnp.max(logits, axis=0, keepdims=True))
    soft = e / jnp.sum(e, axis=0, keepdims=True)
    lb = jnp.zeros((1, width), _F32)
    for r in range(1, layer + 1):
        lb = lb + soft[r:r + 1, :]

    for c0 in range(0, width, PROJ_BLOCK):
        cols = slice(c0, c0 + PROJ_BLOCK)
        q = _silu(proj(width + c0, PROJ_BLOCK))
        qb_scr[:, cols] = q.astype(_BF16)
        sig = jnp.exp(-jnp.log(1.0 + jnp.exp(-proj(2 * width + c0, PROJ_BLOCK))))
        f = lb[:, cols] + (1.0 - lb[:, cols]) * sig
        qf_scr[:, cols] = (q * f).astype(_BF16)
        kb_scr[:, cols] = (1.0 - f).astype(_BF16)
        g = jnp.log(f) * LOG2E
        g_hi = g.astype(_BF16)
        g_lo = (g - g_hi.astype(_F32)).astype(_BF16)
        for c in range(n_chunks):
            gs_scr[c, 0:chunk, cols] = g_hi[c * chunk:(c + 1) * chunk, :]
            gs_scr[c, chunk:2 * chunk, cols] = g_lo[c * chunk:(c + 1) * chunk, :]
        vb_scr[:, cols] = proj(3 * width + c0, PROJ_BLOCK).astype(_BF16)
        so_scr[:, cols] = _silu(proj(4 * width + c0, PROJ_BLOCK))

    u = jnp.concatenate([proj(c0, PROJ_BLOCK) for c0 in range(0, width, PROJ_BLOCK)], axis=1)
    ext = jnp.concatenate([uh_scr[...], u], axis=0)
    uh_scr[...] = u[tm - POOL_HISTORY:, :]
    head_frames = (i * tm + 1 + lax.broadcasted_iota(jnp.int32, (POOL_HISTORY, HEAD_DIM), 0)).astype(_F32)
    sums = ext
    mixed = []
    for gi, w in enumerate(POOL_WINDOWS):
        sums = sums[:, (HEAD_DIM if gi else 0):]
        sums = sums + pltpu.roll(sums, w // 2, 0)
        win = sums[POOL_HISTORY:, 0:HEAD_DIM]
        u_g = u[:, gi * HEAD_DIM:(gi + 1) * HEAD_DIM]
        inv_count = jnp.concatenate([1.0 / jnp.minimum(head_frames, float(w)),
                                     jnp.full((tm - POOL_HISTORY, HEAD_DIM), 1.0 / w, _F32)], axis=0)
        pooled = win * inv_count - u_g
        mixed.append(_dot(pooled.astype(_BF16), pw_ref[gi]))
    mixed = jnp.concatenate(mixed, axis=1) * ps_ref[...]
    ya = _dot(mixed.astype(_BF16), w_pa_ref[...])

    level = lvl_ref[...]

    def tile_of(ref, c, h):
        return ref[c * chunk:(c + 1) * chunk, h * HEAD_DIM:(h + 1) * HEAD_DIM]

    def decay_weights(c):
        expo = _dot(dmat_ref[...], gs_scr[c])
        n_mm = (MATMUL_LEVELS - 1) * chunk
        b = expo[n_mm:, :]
        parts = [expo[:n_mm, :]]
        row = lax.broadcasted_iota(jnp.int32, (chunk, width), 0)
        for l in range(MATMUL_LEVELS, n_levels):
            m = 1 << l
            if 2 * m == SUBLANES:
                ref = jnp.concatenate([jnp.broadcast_to(b[t0 + m - 1:t0 + m, :], (SUBLANES, width))
                                       for t0 in range(0, chunk, SUBLANES)], axis=0)
                parts.append(jnp.where((row & (SUBLANES - 1)) < m, ref - b, b - ref))
                continue
            for g0 in range(0, chunk, 2 * m):
                ref = jnp.broadcast_to(b[g0 + m - 1:g0 + m, :], (m, width))
                parts.append(ref - b[g0:g0 + m, :])
                parts.append(b[g0 + m:g0 + 2 * m, :] - ref)
        parts.append(b)
        parts.append(jnp.broadcast_to(b[chunk - 1:chunk, :], (chunk, width)) - b)
        wgt = jnp.exp2(jnp.concatenate(parts, axis=0)).astype(_BF16)
        w_scr[c] = wgt
        dec_scr[c:c + 1, :] = jnp.exp2(b[chunk - 1:chunk, :])
        return {(l, h): wgt[(l - 1) * chunk:l * chunk, h * HEAD_DIM:(h + 1) * HEAD_DIM].T
                for l in range(1, n_levels) for h in range(n_heads)}

    scores = {}

    def level_scores(c, w_tr):
        k_tr = {}
        for h in range(n_heads):
            kt_scr[c * n_heads + h] = tile_of(kb_scr, c, h).T
            k_tr[h] = kt_scr[c * n_heads + h]
            both = _dot(jnp.concatenate([tile_of(qb_scr, c, h), tile_of(qf_scr, c, h)], axis=0), k_tr[h])
            scores[c, h] = jnp.where(level == 0, both[chunk:, :], jnp.where(level == -1, both[:chunk, :], 0.0))
        for l in range(1, n_levels):
            m = 1 << l
            wgt = w_scr[c, (l - 1) * chunk:l * chunk, :]
            q_c = qb_scr[c * chunk:(c + 1) * chunk, :]
            if m < ROW_PACK:
                q_l = q_c * wgt
                for h in range(n_heads):
                    lanes = slice(h * HEAD_DIM, (h + 1) * HEAD_DIM)
                    scores[c, h] = jnp.where(level == l, _dot(q_l[:, lanes], k_tr[h] * w_tr[l, h]), scores[c, h])
                continue
            upper = [slice(g0 + m, g0 + 2 * m) for g0 in range(0, chunk, 2 * m)]
            q_l = jnp.concatenate([q_c[r, :] * wgt[r, :] for r in upper], axis=0)
            for h in range(n_heads):
                lanes = slice(h * HEAD_DIM, (h + 1) * HEAD_DIM)
                part = _dot(q_l[:, lanes], k_tr[h] * w_tr[l, h])
                old = scores[c, h]
                pieces = []
                for gi, r in enumerate(upper):
                    pieces.append(old[r.start - m:r.start, :])
                    pieces.append(jnp.where(level[r, :] == l, part[gi * m:(gi + 1) * m, :], old[r, :]))
                scores[c, h] = jnp.concatenate(pieces, axis=0)

    def gate_block(c0):
        cols = slice(c0, c0 + PROJ_BLOCK)
        gate_scr[:, cols] = _sigmoid(proj(5 * width + c0, PROJ_BLOCK) + bg_ref[:, cols])

    gate_starts = list(range(0, 2 * d_model, PROJ_BLOCK))
    slots = 2 * n_chunks
    per_slot = -(-len(gate_starts) // slots)

    def fill(slot):
        for c0 in gate_starts[slot * per_slot:(slot + 1) * per_slot]:
            gate_block(c0)

    pending = None
    for c in range(n_chunks):
        w_tr = decay_weights(c)
        fill(2 * c)
        if pending is not None:
            level_scores(*pending)
        fill(2 * c + 1)
        pending = (c, w_tr)
    level_scores(*pending)
    for c0 in gate_starts[slots * per_slot:]:
        gate_block(c0)

    hng = hng_ref[...]
    for c in range(n_chunks):
        rows = slice(c * chunk, (c + 1) * chunk)
        q_in = qb_scr[rows, :] * w_scr[c, (n_levels - 1) * chunk:n_levels * chunk, :]
        k_out = kb_scr[rows, :] * w_scr[c, n_levels * chunk:(n_levels + 1) * chunk, :]
        for h in range(n_heads):
            lanes = slice(h * HEAD_DIM, (h + 1) * HEAD_DIM)
            st = st_scr[h]
            v_t = vb_scr[rows, lanes]
            o = _dot(scores[c, h].astype(_BF16), v_t) + _dot(q_in[:, lanes], st.astype(_BF16))
            st_scr[h] = dec_scr[c:c + 1, lanes].T * st + _dot_tn(k_out[:, lanes], v_t)
            o = o * lax.rsqrt(jnp.mean(o * o, axis=-1, keepdims=True) + EPS) * hng
            o_scr[rows, lanes] = o * so_scr[rows, lanes]
    yb = _dot(o_scr[...].astype(_BF16), w_pb_ref[...])

    merged = gate_scr[:, :d_model] * ya + gate_scr[:, d_model:] * yb
    out_ref[...] = x + _dot(merged.astype(_BF16), w_o_ref[...])


def _ffn_kernel(x_ref, n2g_ref, w_up_ref, cw_ref, cb_ref, w_dn_ref, fg_ref, out_ref, a_scr, tail_scr, *,
                block, final_norm):
    tm, d_model = x_ref.shape
    d_ff = w_dn_ref.shape[0]
    i = pl.program_id(1)

    @pl.when(i == 0)
    def _():
        tail_scr[...] = jnp.zeros_like(tail_scr)

    x = x_ref[...]
    xn = _rmsnorm(x, n2g_ref[...]).astype(_BF16)

    def conv_block(c0, scale):
        h = _dot(xn, w_up_ref[:, c0:c0 + block])
        ext = jnp.concatenate([tail_scr[:, c0:c0 + block], h], axis=0)
        tail_scr[:, c0:c0 + block] = h[tm - CONV_TAIL:, :]
        h1 = pltpu.roll(ext, 1, 0)[CONV_TAIL:, :]
        h2 = pltpu.roll(ext, 2, 0)[CONV_TAIL:, :]
        cw = cw_ref[:, c0:c0 + block] * scale
        return cw[2:3, :] * h + cw[1:2, :] * h1 + cw[0:1, :] * h2 + cb_ref[:, c0:c0 + block] * scale

    for j in range(d_ff // block):
        val = conv_block(j * block, 1.0)
        half_gate = conv_block(d_ff + j * block, 0.5)
        a_scr[:, j * block:(j + 1) * block] = ((half_gate * jnp.tanh(half_gate) + half_gate) * val).astype(_BF16)
    y = x + _dot(a_scr[...], w_dn_ref[...])
    if final_norm:
        y = _rmsnorm(y, fg_ref[...])
    out_ref[...] = y


def _resident(shape):
    zeros = (0,) * len(shape)
    return pl.BlockSpec(shape, lambda b, i: zeros, pipeline_mode=pl.Buffered(1))


def _layer_of(stacked, layer):
    tail = (0,) * (stacked.ndim - 1)
    return pl.BlockSpec((None,) + stacked.shape[1:], lambda b, i: (layer,) + tail, pipeline_mode=pl.Buffered(1))


def _token_tile(seq, largest):
    tm = largest
    while tm >= HGRN_CHUNK:
        if seq % tm == 0:
            return tm
        tm //= 2
    raise ValueError(f"sequence length {seq} must be a multiple of {HGRN_CHUNK}")


def _params():
    return pltpu.CompilerParams(dimension_semantics=("arbitrary", "arbitrary"),
                                vmem_limit_bytes=V7X_VMEM_LIMIT_BYTES)


def _cast_plan(stacked, layer, n_outer, n_inner):
    steps = n_outer * n_inner
    rows = stacked.shape[1]
    n_blocks = 1
    for cand in range(steps, 0, -1):
        if steps % cand == 0 and rows % cand == 0 and (rows // cand) % ROW_PACK == 0:
            n_blocks = cand
            break
    per_block = steps // n_blocks
    block_rows = rows // n_blocks
    src = pl.BlockSpec((None, block_rows, stacked.shape[2]), lambda b, i: (layer, (b * n_inner + i) // per_block, 0))
    dst = pl.BlockSpec((block_rows, stacked.shape[2]), lambda b, i: ((b * n_inner + i) // per_block, 0))
    return src, dst, jax.ShapeDtypeStruct(stacked.shape[1:], _BF16)


def _with_side_casts(body, n_in, n_cast):
    def kernel_fn(*refs):
        inputs = refs[:n_in]
        sources = refs[n_in:n_in + n_cast]
        out_ref = refs[n_in + n_cast]
        targets = refs[n_in + n_cast + 1:n_in + 2 * n_cast + 1]
        scratch = refs[n_in + 2 * n_cast + 1:]
        for src, dst in zip(sources, targets):
            dst[...] = src[...].astype(dst.dtype)
        body(*inputs, out_ref, *scratch)
    return kernel_fn


def _mixer(x, small, weights, lbl, *, layer, convert=()):
    batch, seq, d_model = x.shape
    tm = _token_tile(seq, MIXER_TILE)
    n1g, bg, pw, ps, hng = small
    w_in, w_pa, w_pb, w_o = weights
    width = w_pa.shape[0]
    n_heads = width // HEAD_DIM
    dmat, level, n_levels = _hierarchy_tables(HGRN_CHUNK)
    dmat, level = jnp.asarray(dmat, _BF16), jnp.asarray(level)
    n_chunks = tm // HGRN_CHUNK
    grid = (batch, seq // tm)
    tile = pl.BlockSpec((None, tm, d_model), lambda b, i: (b, i, 0))
    per_layer = functools.partial(_layer_of, layer=layer)
    plans = [_cast_plan(w, layer, *grid) for w in convert]
    inputs = (x, n1g, w_in, bg, pw, ps, lbl, hng, w_pa, w_pb, w_o, dmat, level)
    in_specs = [tile, per_layer(n1g), _resident(w_in.shape), per_layer(bg), per_layer(pw), per_layer(ps),
                _resident(lbl.shape), per_layer(hng), _resident(w_pa.shape), _resident(w_pb.shape),
                _resident(w_o.shape), _resident(dmat.shape), _resident(level.shape)]
    body = functools.partial(_mixer_kernel, layer=layer, n_levels=n_levels)
    return pl.pallas_call(
        _with_side_casts(body, len(inputs), len(convert)),
        out_shape=[jax.ShapeDtypeStruct(x.shape, _F32)] + [p[2] for p in plans],
        grid=grid,
        in_specs=in_specs + [p[0] for p in plans],
        out_specs=[tile] + [p[1] for p in plans],
        scratch_shapes=[
            pltpu.VMEM((tm, width), _BF16),
            pltpu.VMEM((tm, width), _BF16),
            pltpu.VMEM((tm, width), _BF16),
            pltpu.VMEM((tm, width), _BF16),
            pltpu.VMEM((n_chunks, 2 * HGRN_CHUNK, width), _BF16),
            pltpu.VMEM((tm, width), _F32),
            pltpu.VMEM((tm, 2 * d_model), _F32),
            pltpu.VMEM((n_chunks, (n_levels + 1) * HGRN_CHUNK, width), _BF16),
            pltpu.VMEM((8 * ((n_chunks + 7) // 8), width), _F32),
            pltpu.VMEM((tm, width), _F32),
            pltpu.VMEM((n_chunks * n_heads, HEAD_DIM, HEAD_DIM), _BF16),
            pltpu.VMEM((POOL_HISTORY, width), _F32),
            pltpu.VMEM((n_heads, HEAD_DIM, HEAD_DIM), _F32),
        ],
        compiler_params=_params(),
        name=f"mixer_l{layer}",
    )(*inputs, *convert)


def _ffn(x, small, weights, fg, *, layer, final_norm, convert=(), convert_layer=0):
    batch, seq, d_model = x.shape
    tm = _token_tile(seq, FFN_TILE)
    n2g, cw, cb = small
    w_up, w_dn = weights
    d_ff = w_dn.shape[0]
    block = 256 if d_ff % 256 == 0 else 128
    grid = (batch, seq // tm)
    tile = pl.BlockSpec((None, tm, d_model), lambda b, i: (b, i, 0))
    per_layer = functools.partial(_layer_of, layer=layer)
    plans = [_cast_plan(w, convert_layer, *grid) for w in convert]
    inputs = (x, n2g, w_up, cw, cb, w_dn, fg)
    in_specs = [tile, per_layer(n2g), _resident(w_up.shape), per_layer(cw), per_layer(cb), _resident(w_dn.shape),
                _resident(fg.shape)]
    body = functools.partial(_ffn_kernel, block=block, final_norm=final_norm)
    return pl.pallas_call(
        _with_side_casts(body, len(inputs), len(convert)),
        out_shape=[jax.ShapeDtypeStruct(x.shape, _F32)] + [p[2] for p in plans],
        grid=grid,
        in_specs=in_specs + [p[0] for p in plans],
        out_specs=[tile] + [p[1] for p in plans],
        scratch_shapes=[
            pltpu.VMEM((tm, d_ff), _BF16),
            pltpu.VMEM((CONV_TAIL, 2 * d_ff), _F32),
        ],
        compiler_params=_params(),
        name=f"ffn_l{layer}",
    )(*inputs, *convert)


def kernel(x, norm1_g, w_in, b_gate, pool_w, pool_scale, lb_logits, hgrn_norm_g, w_pa, w_pb, w_o, norm2_g,
           w_up, conv_w, conv_b, w_down, final_g):
    depth = w_in.shape[0]
    x = x.astype(_F32)

    def row(a):
        return a.astype(_F32)[:, None, :]

    mixer_small = (row(norm1_g), row(b_gate), pool_w.astype(_BF16), row(pool_scale), row(hgrn_norm_g))
    ffn_small = (row(norm2_g), conv_w.astype(_F32), row(conv_b))
    mixer_f32 = tuple(w.astype(_F32) for w in (w_in, w_pa, w_pb, w_o))
    ffn_f32 = tuple(w.astype(_F32) for w in (w_up, w_down))
    lbl = lb_logits.astype(_F32)
    fg = final_g.astype(_F32)[None, :]
    mixer_w = tuple(w[0].astype(_BF16) for w in mixer_f32)
    for l in range(depth):
        x, *ffn_w = _mixer(x, mixer_small, mixer_w, lbl, layer=l, convert=ffn_f32)
        last = l == depth - 1
        x, *mixer_w = _ffn(x, ffn_small, ffn_w, fg, layer=l, final_norm=last,
                           convert=() if last else mixer_f32, convert_layer=l + 1)
    return x
```

```python
import functools

import numpy as np
import jax
import jax.numpy as jnp
from jax import lax
from jax.experimental import pallas as pl
from jax.experimental.pallas import tpu as pltpu

EPS = 1e-6
POOL_WINDOWS = (2, 4, 8, 16)
POOL_HISTORY = 16
HEAD_DIM = 128
HGRN_CHUNK = 128
CONV_WIDTH = 3
CONV_TAIL = 8
V7X_VMEM_LIMIT_BYTES = 56 * 1024 * 1024
PROJ_BLOCK = 256
MIXER_TILE = 512
FFN_TILE = 1024
LOG2E = 1.4426950408889634
ROW_PACK = 16
SUBLANES = 8
MATMUL_LEVELS = 1

_F32 = jnp.float32
_BF16 = jnp.bfloat16


def _sigmoid(x):
    return 0.5 * jnp.tanh(0.5 * x) + 0.5


def _silu(x):
    h = 0.5 * x
    return h * jnp.tanh(h) + h


def _rmsnorm(x, g):
    return x * lax.rsqrt(jnp.mean(x * x, axis=-1, keepdims=True) + EPS) * g


def _dot(a, b):
    return jnp.dot(a, b, preferred_element_type=_F32)


def _dot_nt(a, b):
    return lax.dot_general(a, b, (((1,), (1,)), ((), ())), preferred_element_type=_F32)


def _dot_tn(a, b):
    return lax.dot_general(a, b, (((0,), (0,)), ((), ())), preferred_element_type=_F32)


def _hierarchy_tables(chunk):
    n_levels = int(np.log2(chunk))
    assert 1 << n_levels == chunk
    t = np.arange(chunk)[:, None]
    u = np.arange(chunk)[None, :]
    mats = []
    for l in range(1, MATMUL_LEVELS):
        m = 1 << l
        p = t % (2 * m)
        upper = p >= m
        block_start = t - p + m
        block_end = t - p + m - 1
        mats.append(np.where(upper, (u >= block_start) & (u <= t), (u > t) & (u <= block_end)))
    mats.append(u <= t)
    dmat = np.concatenate(mats, axis=0).astype(np.float32)
    dmat = np.concatenate([dmat, dmat], axis=1)
    xor = t ^ u
    level = np.full((chunk, chunk), n_levels + 8, np.int32)
    lower = t > u
    level[lower] = np.floor(np.log2(xor[lower])).astype(np.int32)
    level[np.arange(chunk), np.arange(chunk)] = -1
    return dmat, level, n_levels


def _mixer_kernel(x_ref, n1g_ref, w_in_ref, bg_ref, pw_ref, ps_ref, lbl_ref, hng_ref, w_pa_ref, w_pb_ref,
                  w_o_ref, dmat_ref, lvl_ref, out_ref, qb_scr, qf_scr, kb_scr, vb_scr, gs_scr, so_scr, gate_scr,
                  w_scr, dec_scr, o_scr, uh_scr, st_scr, *, layer, n_levels):
    tm, d_model = x_ref.shape
    n_heads = st_scr.shape[0]
    width = n_heads * HEAD_DIM
    chunk = HGRN_CHUNK
    n_chunks = tm // chunk
    i = pl.program_id(1)

    @pl.when(i == 0)
    def _():
        uh_scr[...] = jnp.zeros_like(uh_scr)
        st_scr[...] = jnp.zeros_like(st_scr)

    x = x_ref[...]
    xn = _rmsnorm(x, n1g_ref[...]).astype(_BF16)

    def proj(c0, n):
        return _dot(xn, w_in_ref[:, c0:c0 + n])

    logits = lbl_ref[...]
    e = jnp.exp(logits - jnp.max(logits, axis=0, keepdims=True))
    soft = e / jnp.sum(e, axis=0, keepdims=True)
    lb = jnp.zeros((1, width), _F32)
    for r in range(1, layer + 1):
        lb = lb + soft[r:r + 1, :]

    for c0 in range(0, width, PROJ_BLOCK):
        cols = slice(c0, c0 + PROJ_BLOCK)
        q = _silu(proj(width + c0, PROJ_BLOCK))
        qb_scr[:, cols] = q.astype(_BF16)
        sig = jnp.exp(-jnp.log(1.0 + jnp.exp(-proj(2 * width + c0, PROJ_BLOCK))))
        f = lb[:, cols] + (1.0 - lb[:, cols]) * sig
        qf_scr[:, cols] = (q * f).astype(_BF16)
        kb_scr[:, cols] = (1.0 - f).astype(_BF16)
        g = jnp.log(f) * LOG2E
        g_hi = g.astype(_BF16)
        g_lo = (g - g_hi.astype(_F32)).astype(_BF16)
        for c in range(n_chunks):
            gs_scr[c, 0:chunk, cols] = g_hi[c * chunk:(c + 1) * chunk, :]
            gs_scr[c, chunk:2 * chunk, cols] = g_lo[c * chunk:(c + 1) * chunk, :]
        vb_scr[:, cols] = proj(3 * width + c0, PROJ_BLOCK).astype(_BF16)
        so_scr[:, cols] = _silu(proj(4 * width + c0, PROJ_BLOCK))

    u = jnp.concatenate([proj(c0, PROJ_BLOCK) for c0 in range(0, width, PROJ_BLOCK)], axis=1)
    ext = jnp.concatenate([uh_scr[...], u], axis=0)
    uh_scr[...] = u[tm - POOL_HISTORY:, :]
    head_frames = (i * tm + 1 + lax.broadcasted_iota(jnp.int32, (POOL_HISTORY, HEAD_DIM), 0)).astype(_F32)
    sums = ext
    mixed = []
    for gi, w in enumerate(POOL_WINDOWS):
        sums = sums[:, (HEAD_DIM if gi else 0):]
        sums = sums + pltpu.roll(sums, w // 2, 0)
        win = sums[POOL_HISTORY:, 0:HEAD_DIM]
        u_g = u[:, gi * HEAD_DIM:(gi + 1) * HEAD_DIM]
        inv_count = jnp.concatenate([1.0 / jnp.minimum(head_frames, float(w)),
                                     jnp.full((tm - POOL_HISTORY, HEAD_DIM), 1.0 / w, _F32)], axis=0)
        pooled = win * inv_count - u_g
        mixed.append(_dot(pooled.astype(_BF16), pw_ref[gi]))
    mixed = jnp.concatenate(mixed, axis=1) * ps_ref[...]
    ya = _dot(mixed.astype(_BF16), w_pa_ref[...])

    level = lvl_ref[...]

    def tile_of(ref, c, h):
        return ref[c * chunk:(c + 1) * chunk, h * HEAD_DIM:(h + 1) * HEAD_DIM]

    def decay_weights(c):
        expo = _dot(dmat_ref[...], gs_scr[c])
        n_mm = (MATMUL_LEVELS - 1) * chunk
        b = expo[n_mm:, :]
        parts = [expo[:n_mm, :]] if n_mm else []
        row = lax.broadcasted_iota(jnp.int32, (chunk, width), 0)
        for l in range(MATMUL_LEVELS, n_levels):
            m = 1 << l
            if 2 * m <= SUBLANES:
                ref = None
                for g0 in range(0, SUBLANES, 2 * m):
                    cand = jnp.concatenate([jnp.broadcast_to(b[t0 + g0 + m - 1:t0 + g0 + m, :], (SUBLANES, width))
                                            for t0 in range(0, chunk, SUBLANES)], axis=0)
                    ref = cand if ref is None else jnp.where((row & (SUBLANES - 1)) >= g0, cand, ref)
                parts.append(jnp.where((row & (2 * m - 1)) < m, ref - b, b - ref))
                continue
            for g0 in range(0, chunk, 2 * m):
                ref = jnp.broadcast_to(b[g0 + m - 1:g0 + m, :], (m, width))
                parts.append(ref - b[g0:g0 + m, :])
                parts.append(b[g0 + m:g0 + 2 * m, :] - ref)
        parts.append(b)
        parts.append(jnp.broadcast_to(b[chunk - 1:chunk, :], (chunk, width)) - b)
        wgt = jnp.exp2(jnp.concatenate(parts, axis=0)).astype(_BF16)
        w_scr[c] = wgt
        dec_scr[c:c + 1, :] = jnp.exp2(b[chunk - 1:chunk, :])
        return {(l, h): wgt[(l - 1) * chunk:l * chunk, h * HEAD_DIM:(h + 1) * HEAD_DIM].T
                for l in range(1, n_levels) for h in range(n_heads)}

    scores = {}

    def level_scores(c, w_tr):
        k_tr = {}
        for h in range(n_heads):
            k_t = tile_of(kb_scr, c, h)
            k_tr[h] = k_t.T
            both = _dot_nt(jnp.concatenate([tile_of(qb_scr, c, h), tile_of(qf_scr, c, h)], axis=0), k_t)
            scores[c, h] = jnp.where(level == 0, both[chunk:, :], jnp.where(level == -1, both[:chunk, :], 0.0))
        for l in range(1, n_levels):
            m = 1 << l
            wgt = w_scr[c, (l - 1) * chunk:l * chunk, :]
            q_c = qb_scr[c * chunk:(c + 1) * chunk, :]
            if m < ROW_PACK:
                q_l = q_c * wgt
                for h in range(n_heads):
                    lanes = slice(h * HEAD_DIM, (h + 1) * HEAD_DIM)
                    scores[c, h] = jnp.where(level == l, _dot(q_l[:, lanes], k_tr[h] * w_tr[l, h]), scores[c, h])
                continue
            upper = [slice(g0 + m, g0 + 2 * m) for g0 in range(0, chunk, 2 * m)]
            q_l = jnp.concatenate([q_c[r, :] * wgt[r, :] for r in upper], axis=0)
            for h in range(n_heads):
                lanes = slice(h * HEAD_DIM, (h + 1) * HEAD_DIM)
                part = _dot(q_l[:, lanes], k_tr[h] * w_tr[l, h])
                old = scores[c, h]
                pieces = []
                for gi, r in enumerate(upper):
                    pieces.append(old[r.start - m:r.start, :])
                    pieces.append(jnp.where(level[r, :] == l, part[gi * m:(gi + 1) * m, :], old[r, :]))
                scores[c, h] = jnp.concatenate(pieces, axis=0)

    def gate_block(c0):
        cols = slice(c0, c0 + PROJ_BLOCK)
        gate_scr[:, cols] = _sigmoid(proj(5 * width + c0, PROJ_BLOCK) + bg_ref[:, cols])

    gate_starts = list(range(0, 2 * d_model, PROJ_BLOCK))
    slots = 2 * n_chunks
    per_slot = -(-len(gate_starts) // slots)

    def fill(slot):
        for c0 in gate_starts[slot * per_slot:(slot + 1) * per_slot]:
            gate_block(c0)

    pending = None
    for c in range(n_chunks):
        w_tr = decay_weights(c)
        fill(2 * c)
        if pending is not None:
            level_scores(*pending)
        fill(2 * c + 1)
        pending = (c, w_tr)
    level_scores(*pending)
    for c0 in gate_starts[slots * per_slot:]:
        gate_block(c0)

    hng = hng_ref[...]
    for c in range(n_chunks):
        rows = slice(c * chunk, (c + 1) * chunk)
        q_in = qb_scr[rows, :] * w_scr[c, (n_levels - 1) * chunk:n_levels * chunk, :]
        k_out = kb_scr[rows, :] * w_scr[c, n_levels * chunk:(n_levels + 1) * chunk, :]
        for h in range(n_heads):
            lanes = slice(h * HEAD_DIM, (h + 1) * HEAD_DIM)
            st = st_scr[h]
            v_t = vb_scr[rows, lanes]
            o = _dot(scores[c, h].astype(_BF16), v_t) + _dot(q_in[:, lanes], st.astype(_BF16))
            st_scr[h] = dec_scr[c:c + 1, lanes].T * st + _dot_tn(k_out[:, lanes], v_t)
            o = o * lax.rsqrt(jnp.mean(o * o, axis=-1, keepdims=True) + EPS) * hng
            o_scr[rows, lanes] = o * so_scr[rows, lanes]
    yb = _dot(o_scr[...].astype(_BF16), w_pb_ref[...])

    merged = gate_scr[:, :d_model] * ya + gate_scr[:, d_model:] * yb
    out_ref[...] = x + _dot(merged.astype(_BF16), w_o_ref[...])


def _ffn_kernel(x_ref, n2g_ref, w_up_ref, cw_ref, cb_ref, w_dn_ref, fg_ref, out_ref, a_scr, tail_scr, *,
                block, final_norm):
    tm, d_model = x_ref.shape
    d_ff = w_dn_ref.shape[0]
    i = pl.program_id(1)

    @pl.when(i == 0)
    def _():
        tail_scr[...] = jnp.zeros_like(tail_scr)

    x = x_ref[...]
    xn = _rmsnorm(x, n2g_ref[...]).astype(_BF16)

    def conv_block(c0, scale):
        h = _dot(xn, w_up_ref[:, c0:c0 + block])
        ext = jnp.concatenate([tail_scr[:, c0:c0 + block], h], axis=0)
        tail_scr[:, c0:c0 + block] = h[tm - CONV_TAIL:, :]
        h1 = pltpu.roll(ext, 1, 0)[CONV_TAIL:, :]
        h2 = pltpu.roll(ext, 2, 0)[CONV_TAIL:, :]
        cw = cw_ref[:, c0:c0 + block] * scale
        return cw[2:3, :] * h + cw[1:2, :] * h1 + cw[0:1, :] * h2 + cb_ref[:, c0:c0 + block] * scale

    for j in range(d_ff // block):
        val = conv_block(j * block, 1.0)
        half_gate = conv_block(d_ff + j * block, 0.5)
        a_scr[:, j * block:(j + 1) * block] = ((half_gate * jnp.tanh(half_gate) + half_gate) * val).astype(_BF16)
    y = x + _dot(a_scr[...], w_dn_ref[...])
    if final_norm:
        y = _rmsnorm(y, fg_ref[...])
    out_ref[...] = y


def _resident(shape):
    zeros = (0,) * len(shape)
    return pl.BlockSpec(shape, lambda b, i: zeros, pipeline_mode=pl.Buffered(1))


def _layer_of(stacked, layer):
    tail = (0,) * (stacked.ndim - 1)
    return pl.BlockSpec((None,) + stacked.shape[1:], lambda b, i: (layer,) + tail, pipeline_mode=pl.Buffered(1))


def _token_tile(seq, largest):
    tm = largest
    while tm >= HGRN_CHUNK:
        if seq % tm == 0:
            return tm
        tm //= 2
    raise ValueError(f"sequence length {seq} must be a multiple of {HGRN_CHUNK}")


def _params():
    return pltpu.CompilerParams(dimension_semantics=("arbitrary", "arbitrary"),
                                vmem_limit_bytes=V7X_VMEM_LIMIT_BYTES)


def _cast_plan(stacked, layer, n_outer, n_inner):
    steps = n_outer * n_inner
    rows = stacked.shape[1]
    n_blocks = 1
    for cand in range(steps, 0, -1):
        if steps % cand == 0 and rows % cand == 0 and (rows // cand) % ROW_PACK == 0:
            n_blocks = cand
            break
    per_block = steps // n_blocks
    block_rows = rows // n_blocks
    src = pl.BlockSpec((None, block_rows, stacked.shape[2]), lambda b, i: (layer, (b * n_inner + i) // per_block, 0))
    dst = pl.BlockSpec((block_rows, stacked.shape[2]), lambda b, i: ((b * n_inner + i) // per_block, 0))
    return src, dst, jax.ShapeDtypeStruct(stacked.shape[1:], _BF16)


def _with_side_casts(body, n_in, n_cast):
    def kernel_fn(*refs):
        inputs = refs[:n_in]
        sources = refs[n_in:n_in + n_cast]
        out_ref = refs[n_in + n_cast]
        targets = refs[n_in + n_cast + 1:n_in + 2 * n_cast + 1]
        scratch = refs[n_in + 2 * n_cast + 1:]
        for src, dst in zip(sources, targets):
            dst[...] = src[...].astype(dst.dtype)
        body(*inputs, out_ref, *scratch)
    return kernel_fn


def _mixer(x, small, weights, lbl, *, layer, convert=()):
    batch, seq, d_model = x.shape
    tm = _token_tile(seq, MIXER_TILE)
    n1g, bg, pw, ps, hng = small
    w_in, w_pa, w_pb, w_o = weights
    width = w_pa.shape[0]
    n_heads = width // HEAD_DIM
    dmat, level, n_levels = _hierarchy_tables(HGRN_CHUNK)
    dmat, level = jnp.asarray(dmat, _BF16), jnp.asarray(level)
    n_chunks = tm // HGRN_CHUNK
    grid = (batch, seq // tm)
    tile = pl.BlockSpec((None, tm, d_model), lambda b, i: (b, i, 0))
    per_layer = functools.partial(_layer_of, layer=layer)
    plans = [_cast_plan(w, layer, *grid) for w in convert]
    inputs = (x, n1g, w_in, bg, pw, ps, lbl, hng, w_pa, w_pb, w_o, dmat, level)
    in_specs = [tile, per_layer(n1g), _resident(w_in.shape), per_layer(bg), per_layer(pw), per_layer(ps),
                _resident(lbl.shape), per_layer(hng), _resident(w_pa.shape), _resident(w_pb.shape),
                _resident(w_o.shape), _resident(dmat.shape), _resident(level.shape)]
    body = functools.partial(_mixer_kernel, layer=layer, n_levels=n_levels)
    return pl.pallas_call(
        _with_side_casts(body, len(inputs), len(convert)),
        out_shape=[jax.ShapeDtypeStruct(x.shape, _F32)] + [p[2] for p in plans],
        grid=grid,
        in_specs=in_specs + [p[0] for p in plans],
        out_specs=[tile] + [p[1] for p in plans],
        scratch_shapes=[
            pltpu.VMEM((tm, width), _BF16),
            pltpu.VMEM((tm, width), _BF16),
            pltpu.VMEM((tm, width), _BF16),
            pltpu.VMEM((tm, width), _BF16),
            pltpu.VMEM((n_chunks, 2 * HGRN_CHUNK, width), _BF16),
            pltpu.VMEM((tm, width), _F32),
            pltpu.VMEM((tm, 2 * d_model), _F32),
            pltpu.VMEM((n_chunks, (n_levels + 1) * HGRN_CHUNK, width), _BF16),
            pltpu.VMEM((8 * ((n_chunks + 7) // 8), width), _F32),
            pltpu.VMEM((tm, width), _F32),
            pltpu.VMEM((POOL_HISTORY, width), _F32),
            pltpu.VMEM((n_heads, HEAD_DIM, HEAD_DIM), _F32),
        ],
        compiler_params=_params(),
        name=f"mixer_l{layer}",
    )(*inputs, *convert)


def _ffn(x, small, weights, fg, *, layer, final_norm, convert=(), convert_layer=0):
    batch, seq, d_model = x.shape
    tm = _token_tile(seq, FFN_TILE)
    n2g, cw, cb = small
    w_up, w_dn = weights
    d_ff = w_dn.shape[0]
    block = 256 if d_ff % 256 == 0 else 128
    grid = (batch, seq // tm)
    tile = pl.BlockSpec((None, tm, d_model), lambda b, i: (b, i, 0))
    per_layer = functools.partial(_layer_of, layer=layer)
    plans = [_cast_plan(w, convert_layer, *grid) for w in convert]
    inputs = (x, n2g, w_up, cw, cb, w_dn, fg)
    in_specs = [tile, per_layer(n2g), _resident(w_up.shape), per_layer(cw), per_layer(cb), _resident(w_dn.shape),
                _resident(fg.shape)]
    body = functools.partial(_ffn_kernel, block=block, final_norm=final_norm)
    return pl.pallas_call(
        _with_side_casts(body, len(inputs), len(convert)),
        out_shape=[jax.ShapeDtypeStruct(x.shape, _F32)] + [p[2] for p in plans],
        grid=grid,
        in_specs=in_specs + [p[0] for p in plans],
        out_specs=[tile] + [p[1] for p in plans],
        scratch_shapes=[
            pltpu.VMEM((tm, d_ff), _BF16),
            pltpu.VMEM((CONV_TAIL, 2 * d_ff), _F32),
        ],
        compiler_params=_params(),
        name=f"ffn_l{layer}",
    )(*inputs, *convert)


def kernel(x, norm1_g, w_in, b_gate, pool_w, pool_scale, lb_logits, hgrn_norm_g, w_pa, w_pb, w_o, norm2_g,
           w_up, conv_w, conv_b, w_down, final_g):
    depth = w_in.shape[0]
    x = x.astype(_F32)

    def row(a):
        return a.astype(_F32)[:, None, :]

    mixer_small = (row(norm1_g), row(b_gate), pool_w.astype(_BF16), row(pool_scale), row(hgrn_norm_g))
    ffn_small = (row(norm2_g), conv_w.astype(_F32), row(conv_b))
    mixer_f32 = tuple(w.astype(_F32) for w in (w_in, w_pa, w_pb, w_o))
    ffn_f32 = tuple(w.astype(_F32) for w in (w_up, w_down))
    lbl = lb_logits.astype(_F32)
    fg = final_g.astype(_F32)[None, :]
    mixer_w = tuple(w[0].astype(_BF16) for w in mixer_f32)
    for l in range(depth):
        x, *ffn_w = _mixer(x, mixer_small, mixer_w, lbl, layer=l, convert=ffn_f32)
        last = l == depth - 1
        x, *mixer_w = _ffn(x, ffn_small, ffn_w, fg, layer=l, final_norm=last,
                           convert=() if last else mixer_f32, convert_layer=l + 1)
    return x
```

```python
import functools

import numpy as np
import jax
import jax.numpy as jnp
from jax import lax
from jax.experimental import pallas as pl
from jax.experimental.pallas import tpu as pltpu

EPS = 1e-6
POOL_WINDOWS = (2, 4, 8, 16)
POOL_HISTORY = 16
HEAD_DIM = 128
HGRN_CHUNK = 128
CONV_WIDTH = 3
CONV_TAIL = 8
V7X_VMEM_LIMIT_BYTES = 56 * 1024 * 1024
PROJ_BLOCK = 256
MIXER_TILE = 512
FFN_TILE = 1024
LOG2E = 1.4426950408889634
ROW_PACK = 16
SUBLANES = 8
MATMUL_LEVELS = 1

_F32 = jnp.float32
_BF16 = jnp.bfloat16


def _sigmoid(x):
    return 0.5 * jnp.tanh(0.5 * x) + 0.5


def _silu(x):
    h = 0.5 * x
    return h * jnp.tanh(h) + h


def _rmsnorm(x, g):
    return x * lax.rsqrt(jnp.mean(x * x, axis=-1, keepdims=True) + EPS) * g


def _dot(a, b):
    return jnp.dot(a, b, preferred_element_type=_F32)


def _dot_nt(a, b):
    return lax.dot_general(a, b, (((1,), (1,)), ((), ())), preferred_element_type=_F32)


def _dot_tn(a, b):
    return lax.dot_general(a, b, (((0,), (0,)), ((), ())), preferred_element_type=_F32)


def _hierarchy_tables(chunk):
    n_levels = int(np.log2(chunk))
    assert 1 << n_levels == chunk
    t = np.arange(chunk)[:, None]
    u = np.arange(chunk)[None, :]
    mats = []
    for l in range(1, MATMUL_LEVELS):
        m = 1 << l
        p = t % (2 * m)
        upper = p >= m
        block_start = t - p + m
        block_end = t - p + m - 1
        mats.append(np.where(upper, (u >= block_start) & (u <= t), (u > t) & (u <= block_end)))
    mats.append(u <= t)
    dmat = np.concatenate(mats, axis=0).astype(np.float32)
    dmat = np.concatenate([dmat, dmat], axis=1)
    xor = t ^ u
    level = np.full((chunk, chunk), n_levels + 8, np.int32)
    lower = t > u
    level[lower] = np.floor(np.log2(xor[lower])).astype(np.int32)
    level[np.arange(chunk), np.arange(chunk)] = -1
    return dmat, level, n_levels


def _mixer_kernel(x_ref, n1g_ref, w_in_ref, bg_ref, pw_ref, ps_ref, lbl_ref, hng_ref, w_pa_ref, w_pb_ref,
                  w_o_ref, dmat_ref, lvl_ref, out_ref, qb_scr, qf_scr, kb_scr, vb_scr, gs_scr, so_scr, gate_scr,
                  w_scr, dec_scr, o_scr, kt_scr, uh_scr, st_scr, *, layer, n_levels):
    tm, d_model = x_ref.shape
    n_heads = st_scr.shape[0]
    width = n_heads * HEAD_DIM
    chunk = HGRN_CHUNK
    n_chunks = tm // chunk
    i = pl.program_id(1)

    @pl.when(i == 0)
    def _():
        uh_scr[...] = jnp.zeros_like(uh_scr)
        st_scr[...] = jnp.zeros_like(st_scr)

    x = x_ref[...]
    xn = _rmsnorm(x, n1g_ref[...]).astype(_BF16)

    def proj(c0, n):
        return _dot(xn, w_in_ref[:, c0:c0 + n])

    logits = lbl_ref[...]
    e = jnp.exp(logits - jnp.max(logits, axis=0, keepdims=True))
    soft = e / jnp.sum(e, axis=0, keepdims=True)
    lb = jnp.zeros((1, width), _F32)
    for r in range(1, layer + 1):
        lb = lb + soft[r:r + 1, :]

    for c0 in range(0, width, PROJ_BLOCK):
        cols = slice(c0, c0 + PROJ_BLOCK)
        q = _silu(proj(width + c0, PROJ_BLOCK))
        qb_scr[:, cols] = q.astype(_BF16)
        sig = jnp.exp(-jnp.log(1.0 + jnp.exp(-proj(2 * width + c0, PROJ_BLOCK))))
        f = lb[:, cols] + (1.0 - lb[:, cols]) * sig
        qf_scr[:, cols] = (q * f).astype(_BF16)
        kb_scr[:, cols] = (1.0 - f).astype(_BF16)
        g = jnp.log(f) * LOG2E
        g_hi = g.astype(_BF16)
        g_lo = (g - g_hi.astype(_F32)).astype(_BF16)
        for c in range(n_chunks):
            gs_scr[c, 0:chunk, cols] = g_hi[c * chunk:(c + 1) * chunk, :]
            gs_scr[c, chunk:2 * chunk, cols] = g_lo[c * chunk:(c + 1) * chunk, :]
        vb_scr[:, cols] = proj(3 * width + c0, PROJ_BLOCK).astype(_BF16)
        so_scr[:, cols] = _silu(proj(4 * width + c0, PROJ_BLOCK))

    u = jnp.concatenate([proj(c0, PROJ_BLOCK) for c0 in range(0, width, PROJ_BLOCK)], axis=1)
    ext = jnp.concatenate([uh_scr[...], u], axis=0)
    uh_scr[...] = u[tm - POOL_HISTORY:, :]
    head_frames = (i * tm + 1 + lax.broadcasted_iota(jnp.int32, (POOL_HISTORY, HEAD_DIM), 0)).astype(_F32)
    sums = ext
    mixed = []
    for gi, w in enumerate(POOL_WINDOWS):
        sums = sums[:, (HEAD_DIM if gi else 0):]
        sums = sums + pltpu.roll(sums, w // 2, 0)
        win = sums[POOL_HISTORY:, 0:HEAD_DIM]
        u_g = u[:, gi * HEAD_DIM:(gi + 1) * HEAD_DIM]
        inv_count = jnp.concatenate([1.0 / jnp.minimum(head_frames, float(w)),
                                     jnp.full((tm - POOL_HISTORY, HEAD_DIM), 1.0 / w, _F32)], axis=0)
        pooled = win * inv_count - u_g
        mixed.append(_dot(pooled.astype(_BF16), pw_ref[gi]))
    mixed = jnp.concatenate(mixed, axis=1) * ps_ref[...]
    ya = _dot(mixed.astype(_BF16), w_pa_ref[...])

    level = lvl_ref[...]

    def tile_of(ref, c, h):
        return ref[c * chunk:(c + 1) * chunk, h * HEAD_DIM:(h + 1) * HEAD_DIM]

    def decay_weights(c):
        expo = _dot(dmat_ref[...], gs_scr[c])
        n_mm = (MATMUL_LEVELS - 1) * chunk
        b = expo[n_mm:, :]
        parts = [expo[:n_mm, :]] if n_mm else []
        row = lax.broadcasted_iota(jnp.int32, (chunk, width), 0)
        for l in range(MATMUL_LEVELS, n_levels):
            m = 1 << l
            if 2 * m <= SUBLANES:
                ref = None
                for g0 in range(0, SUBLANES, 2 * m):
                    cand = jnp.concatenate([jnp.broadcast_to(b[t0 + g0 + m - 1:t0 + g0 + m, :], (SUBLANES, width))
                                            for t0 in range(0, chunk, SUBLANES)], axis=0)
                    ref = cand if ref is None else jnp.where((row & (SUBLANES - 1)) >= g0, cand, ref)
                parts.append(jnp.where((row & (2 * m - 1)) < m, ref - b, b - ref))
                continue
            for g0 in range(0, chunk, 2 * m):
                ref = jnp.broadcast_to(b[g0 + m - 1:g0 + m, :], (m, width))
                parts.append(ref - b[g0:g0 + m, :])
                parts.append(b[g0 + m:g0 + 2 * m, :] - ref)
        parts.append(b)
        parts.append(jnp.broadcast_to(b[chunk - 1:chunk, :], (chunk, width)) - b)
        wgt = jnp.exp2(jnp.concatenate(parts, axis=0)).astype(_BF16)
        w_scr[c] = wgt
        dec_scr[c:c + 1, :] = jnp.exp2(b[chunk - 1:chunk, :])
        return {(l, h): wgt[(l - 1) * chunk:l * chunk, h * HEAD_DIM:(h + 1) * HEAD_DIM].T
                for l in range(1, n_levels) for h in range(n_heads)}

    scores = {}

    def level_scores(c, w_tr):
        k_tr = {}
        for h in range(n_heads):
            kt_scr[c * n_heads + h] = tile_of(kb_scr, c, h).T
            k_tr[h] = kt_scr[c * n_heads + h]
            both = _dot(jnp.concatenate([tile_of(qb_scr, c, h), tile_of(qf_scr, c, h)], axis=0), k_tr[h])
            scores[c, h] = jnp.where(level == 0, both[chunk:, :], jnp.where(level == -1, both[:chunk, :], 0.0))
        for l in range(1, n_levels):
            m = 1 << l
            wgt = w_scr[c, (l - 1) * chunk:l * chunk, :]
            q_c = qb_scr[c * chunk:(c + 1) * chunk, :]
            if m < ROW_PACK:
                q_l = q_c * wgt
                for h in range(n_heads):
                    lanes = slice(h * HEAD_DIM, (h + 1) * HEAD_DIM)
                    scores[c, h] = jnp.where(level == l, _dot(q_l[:, lanes], k_tr[h] * w_tr[l, h]), scores[c, h])
                continue
            upper = [slice(g0 + m, g0 + 2 * m) for g0 in range(0, chunk, 2 * m)]
            q_l = jnp.concatenate([q_c[r, :] * wgt[r, :] for r in upper], axis=0)
            for h in range(n_heads):
                lanes = slice(h * HEAD_DIM, (h + 1) * HEAD_DIM)
                part = _dot(q_l[:, lanes], k_tr[h] * w_tr[l, h])
                old = scores[c, h]
                pieces = []
                for gi, r in enumerate(upper):
                    pieces.append(old[r.start - m:r.start, :])
                    pieces.append(jnp.where(level[r, :] == l, part[gi * m:(gi + 1) * m, :], old[r, :]))
                scores[c, h] = jnp.concatenate(pieces, axis=0)

    def gate_block(c0):
        cols = slice(c0, c0 + PROJ_BLOCK)
        gate_scr[:, cols] = _sigmoid(proj(5 * width + c0, PROJ_BLOCK) + bg_ref[:, cols])

    gate_starts = list(range(0, 2 * d_model, PROJ_BLOCK))
    slots = 2 * n_chunks
    per_slot = -(-len(gate_starts) // slots)

    def fill(slot):
        for c0 in gate_starts[slot * per_slot:(slot + 1) * per_slot]:
            gate_block(c0)

    pending = None
    for c in range(n_chunks):
        w_tr = decay_weights(c)
        fill(2 * c)
        if pending is not None:
            level_scores(*pending)
        fill(2 * c + 1)
        pending = (c, w_tr)
    level_scores(*pending)
    for c0 in gate_starts[slots * per_slot:]:
        gate_block(c0)

    hng = hng_ref[...]
    for c in range(n_chunks):
        rows = slice(c * chunk, (c + 1) * chunk)
        q_in = qb_scr[rows, :] * w_scr[c, (n_levels - 1) * chunk:n_levels * chunk, :]
        k_out = kb_scr[rows, :] * w_scr[c, n_levels * chunk:(n_levels + 1) * chunk, :]
        for h in range(n_heads):
            lanes = slice(h * HEAD_DIM, (h + 1) * HEAD_DIM)
            st = st_scr[h]
            v_t = vb_scr[rows, lanes]
            o = _dot(scores[c, h].astype(_BF16), v_t) + _dot(q_in[:, lanes], st.astype(_BF16))
            st_scr[h] = dec_scr[c:c + 1, lanes].T * st + _dot_tn(k_out[:, lanes], v_t)
            o = o * lax.rsqrt(jnp.mean(o * o, axis=-1, keepdims=True) + EPS) * hng
            o_scr[rows, lanes] = o * so_scr[rows, lanes]
    yb = _dot(o_scr[...].astype(_BF16), w_pb_ref[...])

    merged = gate_scr[:, :d_model] * ya + gate_scr[:, d_model:] * yb
    out_ref[...] = x + _dot(merged.astype(_BF16), w_o_ref[...])


def _ffn_kernel(x_ref, n2g_ref, w_up_ref, cw_ref, cb_ref, w_dn_ref, fg_ref, out_ref, a_scr, tail_scr, *,
                block, final_norm):
    tm, d_model = x_ref.shape
    d_ff = w_dn_ref.shape[0]
    i = pl.program_id(1)

    @pl.when(i == 0)
    def _():
        tail_scr[...] = jnp.zeros_like(tail_scr)

    x = x_ref[...]
    xn = _rmsnorm(x, n2g_ref[...]).astype(_BF16)

    def conv_block(c0, scale):
        h = _dot(xn, w_up_ref[:, c0:c0 + block])
        ext = jnp.concatenate([tail_scr[:, c0:c0 + block], h], axis=0)
        tail_scr[:, c0:c0 + block] = h[tm - CONV_TAIL:, :]
        h1 = pltpu.roll(ext, 1, 0)[CONV_TAIL:, :]
        h2 = pltpu.roll(ext, 2, 0)[CONV_TAIL:, :]
        cw = cw_ref[:, c0:c0 + block] * scale
        return cw[2:3, :] * h + cw[1:2, :] * h1 + cw[0:1, :] * h2 + cb_ref[:, c0:c0 + block] * scale

    for j in range(d_ff // block):
        val = conv_block(j * block, 1.0)
        half_gate = conv_block(d_ff + j * block, 0.5)
        a_scr[:, j * block:(j + 1) * block] = ((half_gate * jnp.tanh(half_gate) + half_gate) * val).astype(_BF16)
    y = x + _dot(a_scr[...], w_dn_ref[...])
    if final_norm:
        y = _rmsnorm(y, fg_ref[...])
    out_ref[...] = y


def _resident(shape):
    zeros = (0,) * len(shape)
    return pl.BlockSpec(shape, lambda b, i: zeros, pipeline_mode=pl.Buffered(1))


def _layer_of(stacked, layer):
    tail = (0,) * (stacked.ndim - 1)
    return pl.BlockSpec((None,) + stacked.shape[1:], lambda b, i: (layer,) + tail, pipeline_mode=pl.Buffered(1))


def _token_tile(seq, largest):
    tm = largest
    while tm >= HGRN_CHUNK:
        if seq % tm == 0:
            return tm
        tm //= 2
    raise ValueError(f"sequence length {seq} must be a multiple of {HGRN_CHUNK}")


def _params():
    return pltpu.CompilerParams(dimension_semantics=("arbitrary", "arbitrary"),
                                vmem_limit_bytes=V7X_VMEM_LIMIT_BYTES)


def _cast_plan(stacked, layer, n_outer, n_inner):
    steps = n_outer * n_inner
    rows = stacked.shape[1]
    n_blocks = 1
    for cand in range(steps, 0, -1):
        if steps % cand == 0 and rows % cand == 0 and (rows // cand) % ROW_PACK == 0:
            n_blocks = cand
            break
    per_block = steps // n_blocks
    block_rows = rows // n_blocks
    src = pl.BlockSpec((None, block_rows, stacked.shape[2]), lambda b, i: (layer, (b * n_inner + i) // per_block, 0))
    dst = pl.BlockSpec((block_rows, stacked.shape[2]), lambda b, i: ((b * n_inner + i) // per_block, 0))
    return src, dst, jax.ShapeDtypeStruct(stacked.shape[1:], _BF16)


def _with_side_casts(body, n_in, n_cast):
    def kernel_fn(*refs):
        inputs = refs[:n_in]
        sources = refs[n_in:n_in + n_cast]
        out_ref = refs[n_in + n_cast]
        targets = refs[n_in + n_cast + 1:n_in + 2 * n_cast + 1]
        scratch = refs[n_in + 2 * n_cast + 1:]
        for src, dst in zip(sources, targets):
            dst[...] = src[...].astype(dst.dtype)
        body(*inputs, out_ref, *scratch)
    return kernel_fn


def _mixer(x, small, weights, lbl, *, layer, convert=()):
    batch, seq, d_model = x.shape
    tm = _token_tile(seq, MIXER_TILE)
    n1g, bg, pw, ps, hng = small
    w_in, w_pa, w_pb, w_o = weights
    width = w_pa.shape[0]
    n_heads = width // HEAD_DIM
    dmat, level, n_levels = _hierarchy_tables(HGRN_CHUNK)
    dmat, level = jnp.asarray(dmat, _BF16), jnp.asarray(level)
    n_chunks = tm // HGRN_CHUNK
    grid = (batch, seq // tm)
    tile = pl.BlockSpec((None, tm, d_model), lambda b, i: (b, i, 0))
    per_layer = functools.partial(_layer_of, layer=layer)
    plans = [_cast_plan(w, layer, *grid) for w in convert]
    inputs = (x, n1g, w_in, bg, pw, ps, lbl, hng, w_pa, w_pb, w_o, dmat, level)
    in_specs = [tile, per_layer(n1g), _resident(w_in.shape), per_layer(bg), per_layer(pw), per_layer(ps),
                _resident(lbl.shape), per_layer(hng), _resident(w_pa.shape), _resident(w_pb.shape),
                _resident(w_o.shape), _resident(dmat.shape), _resident(level.shape)]
    body = functools.partial(_mixer_kernel, layer=layer, n_levels=n_levels)
    return pl.pallas_call(
        _with_side_casts(body, len(inputs), len(convert)),
        out_shape=[jax.ShapeDtypeStruct(x.shape, _F32)] + [p[2] for p in plans],
        grid=grid,
        in_specs=in_specs + [p[0] for p in plans],
        out_specs=[tile] + [p[1] for p in plans],
        scratch_shapes=[
            pltpu.VMEM((tm, width), _BF16),
            pltpu.VMEM((tm, width), _BF16),
            pltpu.VMEM((tm, width), _BF16),
            pltpu.VMEM((tm, width), _BF16),
            pltpu.VMEM((n_chunks, 2 * HGRN_CHUNK, width), _BF16),
            pltpu.VMEM((tm, width), _F32),
            pltpu.VMEM((tm, 2 * d_model), _F32),
            pltpu.VMEM((n_chunks, (n_levels + 1) * HGRN_CHUNK, width), _BF16),
            pltpu.VMEM((8 * ((n_chunks + 7) // 8), width), _F32),
            pltpu.VMEM((tm, width), _F32),
            pltpu.VMEM((n_chunks * n_heads, HEAD_DIM, HEAD_DIM), _BF16),
            pltpu.VMEM((POOL_HISTORY, width), _F32),
            pltpu.VMEM((n_heads, HEAD_DIM, HEAD_DIM), _F32),
        ],
        compiler_params=_params(),
        name=f"mixer_l{layer}",
    )(*inputs, *convert)


def _ffn(x, small, weights, fg, *, layer, final_norm, convert=(), convert_layer=0):
    batch, seq, d_model = x.shape
    tm = _token_tile(seq, FFN_TILE)
    n2g, cw, cb = small
    w_up, w_dn = weights
    d_ff = w_dn.shape[0]
    block = 256 if d_ff % 256 == 0 else 128
    grid = (batch, seq // tm)
    tile = pl.BlockSpec((None, tm, d_model), lambda b, i: (b, i, 0))
    per_layer = functools.partial(_layer_of, layer=layer)
    plans = [_cast_plan(w, convert_layer, *grid) for w in convert]
    inputs = (x, n2g, w_up, cw, cb, w_dn, fg)
    in_specs = [tile, per_layer(n2g), _resident(w_up.shape), per_layer(cw), per_layer(cb), _resident(w_dn.shape),
                _resident(fg.shape)]
    body = functools.partial(_ffn_kernel, block=block, final_norm=final_norm)
    return pl.pallas_call(
        _with_side_casts(body, len(inputs), len(convert)),
        out_shape=[jax.ShapeDtypeStruct(x.shape, _F32)] + [p[2] for p in plans],
        grid=grid,
        in_specs=in_specs + [p[0] for p in plans],
        out_specs=[tile] + [p[1] for p in plans],
        scratch_shapes=[
            pltpu.VMEM((tm, d_ff), _BF16),
            pltpu.VMEM((CONV_TAIL, 2 * d_ff), _F32),
        ],
        compiler_params=_params(),
        name=f"ffn_l{layer}",
    )(*inputs, *convert)


def kernel(x, norm1_g, w_in, b_gate, pool_w, pool_scale, lb_logits, hgrn_norm_g, w_pa, w_pb, w_o, norm2_g,
           w_up, conv_w, conv_b, w_down, final_g):
    depth = w_in.shape[0]
    x = x.astype(_F32)

    def row(a):
        return a.astype(_F32)[:, None, :]

    mixer_small = (row(norm1_g), row(b_gate), pool_w.astype(_BF16), row(pool_scale), row(hgrn_norm_g))
    ffn_small = (row(norm2_g), conv_w.astype(_F32), row(conv_b))
    mixer_f32 = tuple(w.astype(_F32) for w in (w_in, w_pa, w_pb, w_o))
    ffn_f32 = tuple(w.astype(_F32) for w in (w_up, w_down))
    lbl = lb_logits.astype(_F32)
    fg = final_g.astype(_F32)[None, :]
    mixer_w = tuple(w[0].astype(_BF16) for w in mixer_f32)
    for l in range(depth):
        x, *ffn_w = _mixer(x, mixer_small, mixer_w, lbl, layer=l, convert=ffn_f32)
        last = l == depth - 1
        x, *mixer_w = _ffn(x, ffn_small, ffn_w, fg, layer=l, final_norm=last,
                           convert=() if last else mixer_f32, convert_layer=l + 1)
    return x
```

```python
import functools

import numpy as np
import jax
import jax.numpy as jnp
from jax import lax
from jax.experimental import pallas as pl
from jax.experimental.pallas import tpu as pltpu

EPS = 1e-6
POOL_WINDOWS = (2, 4, 8, 16)
POOL_HISTORY = 16
HEAD_DIM = 128
HGRN_CHUNK = 128
CONV_WIDTH = 3
CONV_TAIL = 8
V7X_VMEM_LIMIT_BYTES = 56 * 1024 * 1024
PROJ_BLOCK = 256
MIXER_TILE = 512
FFN_TILE = 1024
LOG2E = 1.4426950408889634
ROW_PACK = 16
SUBLANES = 8
MATMUL_LEVELS = 1

_F32 = jnp.float32
_BF16 = jnp.bfloat16


def _sigmoid(x):
    return 0.5 * jnp.tanh(0.5 * x) + 0.5


def _silu(x):
    h = 0.5 * x
    return h * jnp.tanh(h) + h


def _rmsnorm(x, g):
    return x * lax.rsqrt(jnp.mean(x * x, axis=-1, keepdims=True) + EPS) * g


def _dot(a, b):
    return jnp.dot(a, b, preferred_element_type=_F32)


def _dot_nt(a, b):
    return lax.dot_general(a, b, (((1,), (1,)), ((), ())), preferred_element_type=_F32)


def _dot_tn(a, b):
    return lax.dot_general(a, b, (((0,), (0,)), ((), ())), preferred_element_type=_F32)


def _hierarchy_tables(chunk):
    n_levels = int(np.log2(chunk))
    assert 1 << n_levels == chunk
    t = np.arange(chunk)[:, None]
    u = np.arange(chunk)[None, :]
    mats = []
    for l in range(1, MATMUL_LEVELS):
        m = 1 << l
        p = t % (2 * m)
        upper = p >= m
        block_start = t - p + m
        block_end = t - p + m - 1
        mats.append(np.where(upper, (u >= block_start) & (u <= t), (u > t) & (u <= block_end)))
    mats.append(u <= t)
    dmat = np.concatenate(mats, axis=0).astype(np.float32)
    dmat = np.concatenate([dmat, dmat], axis=1)
    xor = t ^ u
    level = np.full((chunk, chunk), n_levels + 8, np.int32)
    lower = t > u
    level[lower] = np.floor(np.log2(xor[lower])).astype(np.int32)
    level[np.arange(chunk), np.arange(chunk)] = -1
    return dmat, level, n_levels


def _mixer_kernel(x_ref, n1g_ref, w_in_ref, bg_ref, pw_ref, ps_ref, lbl_ref, hng_ref, w_pa_ref, w_pb_ref,
                  w_o_ref, dmat_ref, lvl_ref, out_ref, qb_scr, qf_scr, kb_scr, vb_scr, gs_scr, so_scr, gate_scr,
                  w_scr, dec_scr, o_scr, uh_scr, st_scr, *, layer, n_levels):
    tm, d_model = x_ref.shape
    n_heads = st_scr.shape[0]
    width = n_heads * HEAD_DIM
    chunk = HGRN_CHUNK
    n_chunks = tm // chunk
    i = pl.program_id(1)

    @pl.when(i == 0)
    def _():
        uh_scr[...] = jnp.zeros_like(uh_scr)
        st_scr[...] = jnp.zeros_like(st_scr)

    x = x_ref[...]
    xn = _rmsnorm(x, n1g_ref[...]).astype(_BF16)

    def proj(c0, n):
        return _dot(xn, w_in_ref[:, c0:c0 + n])

    logits = lbl_ref[...]
    e = jnp.exp(logits - jnp.max(logits, axis=0, keepdims=True))
    soft = e / jnp.sum(e, axis=0, keepdims=True)
    lb = jnp.zeros((1, width), _F32)
    for r in range(1, layer + 1):
        lb = lb + soft[r:r + 1, :]

    for c0 in range(0, width, PROJ_BLOCK):
        cols = slice(c0, c0 + PROJ_BLOCK)
        q = _silu(proj(width + c0, PROJ_BLOCK))
        qb_scr[:, cols] = q.astype(_BF16)
        sig = jnp.exp(-jnp.log(1.0 + jnp.exp(-proj(2 * width + c0, PROJ_BLOCK))))
        f = lb[:, cols] + (1.0 - lb[:, cols]) * sig
        qf_scr[:, cols] = (q * f).astype(_BF16)
        kb_scr[:, cols] = (1.0 - f).astype(_BF16)
        g = jnp.log(f) * LOG2E
        g_hi = g.astype(_BF16)
        g_lo = (g - g_hi.astype(_F32)).astype(_BF16)
        for c in range(n_chunks):
            gs_scr[c, 0:chunk, cols] = g_hi[c * chunk:(c + 1) * chunk, :]
            gs_scr[c, chunk:2 * chunk, cols] = g_lo[c * chunk:(c + 1) * chunk, :]
        vb_scr[:, cols] = proj(3 * width + c0, PROJ_BLOCK).astype(_BF16)
        so_scr[:, cols] = _silu(proj(4 * width + c0, PROJ_BLOCK))

    u = jnp.concatenate([proj(c0, PROJ_BLOCK) for c0 in range(0, width, PROJ_BLOCK)], axis=1)
    ext = jnp.concatenate([uh_scr[...], u], axis=0)
    uh_scr[...] = u[tm - POOL_HISTORY:, :]
    head_frames = (i * tm + 1 + lax.broadcasted_iota(jnp.int32, (POOL_HISTORY, HEAD_DIM), 0)).astype(_F32)
    sums = ext
    mixed = []
    for gi, w in enumerate(POOL_WINDOWS):
        sums = sums[:, (HEAD_DIM if gi else 0):]
        sums = sums + pltpu.roll(sums, w // 2, 0)
        win = sums[POOL_HISTORY:, 0:HEAD_DIM]
        u_g = u[:, gi * HEAD_DIM:(gi + 1) * HEAD_DIM]
        inv_count = jnp.concatenate([1.0 / jnp.minimum(head_frames, float(w)),
                                     jnp.full((tm - POOL_HISTORY, HEAD_DIM), 1.0 / w, _F32)], axis=0)
        pooled = win * inv_count - u_g
        mixed.append(_dot(pooled.astype(_BF16), pw_ref[gi]))
    mixed = jnp.concatenate(mixed, axis=1) * ps_ref[...]
    ya = _dot(mixed.astype(_BF16), w_pa_ref[...])

    level = lvl_ref[...]

    def tile_of(ref, c, h):
        return ref[c * chunk:(c + 1) * chunk, h * HEAD_DIM:(h + 1) * HEAD_DIM]

    def decay_weights(c):
        expo = _dot(dmat_ref[...], gs_scr[c])
        n_mm = (MATMUL_LEVELS - 1) * chunk
        b = expo[n_mm:, :]
        parts = [expo[:n_mm, :]] if n_mm else []
        row = lax.broadcasted_iota(jnp.int32, (chunk, width), 0)
        for l in range(MATMUL_LEVELS, n_levels):
            m = 1 << l
            if 2 * m <= SUBLANES:
                ref = None
                for g0 in range(0, SUBLANES, 2 * m):
                    cand = jnp.concatenate([jnp.broadcast_to(b[t0 + g0 + m - 1:t0 + g0 + m, :], (SUBLANES, width))
                                            for t0 in range(0, chunk, SUBLANES)], axis=0)
                    ref = cand if ref is None else jnp.where((row & (SUBLANES - 1)) >= g0, cand, ref)
                parts.append(jnp.where((row & (2 * m - 1)) < m, ref - b, b - ref))
                continue
            for g0 in range(0, chunk, 2 * m):
                ref = jnp.broadcast_to(b[g0 + m - 1:g0 + m, :], (m, width))
                parts.append(ref - b[g0:g0 + m, :])
                parts.append(b[g0 + m:g0 + 2 * m, :] - ref)
        parts.append(b)
        parts.append(jnp.broadcast_to(b[chunk - 1:chunk, :], (chunk, width)) - b)
        wgt = jnp.exp2(jnp.concatenate(parts, axis=0)).astype(_BF16)
        w_scr[c] = wgt
        dec_scr[c:c + 1, :] = jnp.exp2(b[chunk - 1:chunk, :])
        return {(l, h): wgt[(l - 1) * chunk:l * chunk, h * HEAD_DIM:(h + 1) * HEAD_DIM].T
                for l in range(1, n_levels) for h in range(n_heads)}

    scores = {}

    def level_scores(c, w_tr):
        k_tr = {}
        for h in range(n_heads):
            k_t = tile_of(kb_scr, c, h)
            k_tr[h] = k_t.T
            both = _dot_nt(jnp.concatenate([tile_of(qb_scr, c, h), tile_of(qf_scr, c, h)], axis=0), k_t)
            scores[c, h] = jnp.where(level == 0, both[chunk:, :], jnp.where(level == -1, both[:chunk, :], 0.0))
        for l in range(1, n_levels):
            m = 1 << l
            wgt = w_scr[c, (l - 1) * chunk:l * chunk, :]
            q_c = qb_scr[c * chunk:(c + 1) * chunk, :]
            if m < ROW_PACK:
                q_l = q_c * wgt
                for h in range(n_heads):
                    lanes = slice(h * HEAD_DIM, (h + 1) * HEAD_DIM)
                    scores[c, h] = jnp.where(level == l, _dot(q_l[:, lanes], k_tr[h] * w_tr[l, h]), scores[c, h])
                continue
            upper = [slice(g0 + m, g0 + 2 * m) for g0 in range(0, chunk, 2 * m)]
            q_l = jnp.concatenate([q_c[r, :] * wgt[r, :] for r in upper], axis=0)
            for h in range(n_heads):
                lanes = slice(h * HEAD_DIM, (h + 1) * HEAD_DIM)
                part = _dot(q_l[:, lanes], k_tr[h] * w_tr[l, h])
                old = scores[c, h]
                pieces = []
                for gi, r in enumerate(upper):
                    pieces.append(old[r.start - m:r.start, :])
                    pieces.append(jnp.where(level[r, :] == l, part[gi * m:(gi + 1) * m, :], old[r, :]))
                scores[c, h] = jnp.concatenate(pieces, axis=0)

    def gate_block(c0):
        cols = slice(c0, c0 + PROJ_BLOCK)
        gate_scr[:, cols] = _sigmoid(proj(5 * width + c0, PROJ_BLOCK) + bg_ref[:, cols])

    gate_starts = list(range(0, 2 * d_model, PROJ_BLOCK))
    slots = 2 * n_chunks
    per_slot = -(-len(gate_starts) // slots)

    def fill(slot):
        for c0 in gate_starts[slot * per_slot:(slot + 1) * per_slot]:
            gate_block(c0)

    pending = None
    for c in range(n_chunks):
        w_tr = decay_weights(c)
        fill(2 * c)
        if pending is not None:
            level_scores(*pending)
        fill(2 * c + 1)
        pending = (c, w_tr)
    level_scores(*pending)
    for c0 in gate_starts[slots * per_slot:]:
        gate_block(c0)

    hng = hng_ref[...]
    for c in range(n_chunks):
        rows = slice(c * chunk, (c + 1) * chunk)
        q_in = qb_scr[rows, :] * w_scr[c, (n_levels - 1) * chunk:n_levels * chunk, :]
        k_out = kb_scr[rows, :] * w_scr[c, n_levels * chunk:(n_levels + 1) * chunk, :]
        for h in range(n_heads):
            lanes = slice(h * HEAD_DIM, (h + 1) * HEAD_DIM)
            st = st_scr[h]
            v_t = vb_scr[rows, lanes]
            o = _dot(jnp.concatenate([scores[c, h].astype(_BF16), q_in[:, lanes]], axis=1),
                     jnp.concatenate([v_t, st.astype(_BF16)], axis=0))
            st_scr[h] = dec_scr[c:c + 1, lanes].T * st + _dot_tn(k_out[:, lanes], v_t)
            o = o * lax.rsqrt(jnp.mean(o * o, axis=-1, keepdims=True) + EPS) * hng
            o_scr[rows, lanes] = o * so_scr[rows, lanes]
    yb = _dot(o_scr[...].astype(_BF16), w_pb_ref[...])

    merged = gate_scr[:, :d_model] * ya + gate_scr[:, d_model:] * yb
    out_ref[...] = x + _dot(merged.astype(_BF16), w_o_ref[...])


def _ffn_kernel(x_ref, n2g_ref, w_up_ref, cw_ref, cb_ref, w_dn_ref, fg_ref, out_ref, a_scr, tail_scr, *,
                block, final_norm):
    tm, d_model = x_ref.shape
    d_ff = w_dn_ref.shape[0]
    i = pl.program_id(1)

    @pl.when(i == 0)
    def _():
        tail_scr[...] = jnp.zeros_like(tail_scr)

    x = x_ref[...]
    xn = _rmsnorm(x, n2g_ref[...]).astype(_BF16)

    def conv_block(c0, scale):
        h = _dot(xn, w_up_ref[:, c0:c0 + block])
        ext = jnp.concatenate([tail_scr[:, c0:c0 + block], h], axis=0)
        tail_scr[:, c0:c0 + block] = h[tm - CONV_TAIL:, :]
        h1 = pltpu.roll(ext, 1, 0)[CONV_TAIL:, :]
        h2 = pltpu.roll(ext, 2, 0)[CONV_TAIL:, :]
        cw = cw_ref[:, c0:c0 + block] * scale
        return cw[2:3, :] * h + cw[1:2, :] * h1 + cw[0:1, :] * h2 + cb_ref[:, c0:c0 + block] * scale

    for j in range(d_ff // block):
        val = conv_block(j * block, 1.0)
        half_gate = conv_block(d_ff + j * block, 0.5)
        a_scr[:, j * block:(j + 1) * block] = ((half_gate * jnp.tanh(half_gate) + half_gate) * val).astype(_BF16)
    y = x + _dot(a_scr[...], w_dn_ref[...])
    if final_norm:
        y = _rmsnorm(y, fg_ref[...])
    out_ref[...] = y


def _resident(shape):
    zeros = (0,) * len(shape)
    return pl.BlockSpec(shape, lambda b, i: zeros, pipeline_mode=pl.Buffered(1))


def _layer_of(stacked, layer):
    tail = (0,) * (stacked.ndim - 1)
    return pl.BlockSpec((None,) + stacked.shape[1:], lambda b, i: (layer,) + tail, pipeline_mode=pl.Buffered(1))


def _token_tile(seq, largest):
    tm = largest
    while tm >= HGRN_CHUNK:
        if seq % tm == 0:
            return tm
        tm //= 2
    raise ValueError(f"sequence length {seq} must be a multiple of {HGRN_CHUNK}")


def _params():
    return pltpu.CompilerParams(dimension_semantics=("arbitrary", "arbitrary"),
                                vmem_limit_bytes=V7X_VMEM_LIMIT_BYTES)


def _cast_plan(stacked, layer, n_outer, n_inner):
    steps = n_outer * n_inner
    rows = stacked.shape[1]
    n_blocks = 1
    for cand in range(steps, 0, -1):
        if steps % cand == 0 and rows % cand == 0 and (rows // cand) % ROW_PACK == 0:
            n_blocks = cand
            break
    per_block = steps // n_blocks
    block_rows = rows // n_blocks
    src = pl.BlockSpec((None, block_rows, stacked.shape[2]), lambda b, i: (layer, (b * n_inner + i) // per_block, 0))
    dst = pl.BlockSpec((block_rows, stacked.shape[2]), lambda b, i: ((b * n_inner + i) // per_block, 0))
    return src, dst, jax.ShapeDtypeStruct(stacked.shape[1:], _BF16)


def _with_side_casts(body, n_in, n_cast):
    def kernel_fn(*refs):
        inputs = refs[:n_in]
        sources = refs[n_in:n_in + n_cast]
        out_ref = refs[n_in + n_cast]
        targets = refs[n_in + n_cast + 1:n_in + 2 * n_cast + 1]
        scratch = refs[n_in + 2 * n_cast + 1:]
        for src, dst in zip(sources, targets):
            dst[...] = src[...].astype(dst.dtype)
        body(*inputs, out_ref, *scratch)
    return kernel_fn


def _mixer(x, small, weights, lbl, *, layer, convert=()):
    batch, seq, d_model = x.shape
    tm = _token_tile(seq, MIXER_TILE)
    n1g, bg, pw, ps, hng = small
    w_in, w_pa, w_pb, w_o = weights
    width = w_pa.shape[0]
    n_heads = width // HEAD_DIM
    dmat, level, n_levels = _hierarchy_tables(HGRN_CHUNK)
    dmat, level = jnp.asarray(dmat, _BF16), jnp.asarray(level)
    n_chunks = tm // HGRN_CHUNK
    grid = (batch, seq // tm)
    tile = pl.BlockSpec((None, tm, d_model), lambda b, i: (b, i, 0))
    per_layer = functools.partial(_layer_of, layer=layer)
    plans = [_cast_plan(w, layer, *grid) for w in convert]
    inputs = (x, n1g, w_in, bg, pw, ps, lbl, hng, w_pa, w_pb, w_o, dmat, level)
    in_specs = [tile, per_layer(n1g), _resident(w_in.shape), per_layer(bg), per_layer(pw), per_layer(ps),
                _resident(lbl.shape), per_layer(hng), _resident(w_pa.shape), _resident(w_pb.shape),
                _resident(w_o.shape), _resident(dmat.shape), _resident(level.shape)]
    body = functools.partial(_mixer_kernel, layer=layer, n_levels=n_levels)
    return pl.pallas_call(
        _with_side_casts(body, len(inputs), len(convert)),
        out_shape=[jax.ShapeDtypeStruct(x.shape, _F32)] + [p[2] for p in plans],
        grid=grid,
        in_specs=in_specs + [p[0] for p in plans],
        out_specs=[tile] + [p[1] for p in plans],
        scratch_shapes=[
            pltpu.VMEM((tm, width), _BF16),
            pltpu.VMEM((tm, width), _BF16),
            pltpu.VMEM((tm, width), _BF16),
            pltpu.VMEM((tm, width), _BF16),
            pltpu.VMEM((n_chunks, 2 * HGRN_CHUNK, width), _BF16),
            pltpu.VMEM((tm, width), _F32),
            pltpu.VMEM((tm, 2 * d_model), _F32),
            pltpu.VMEM((n_chunks, (n_levels + 1) * HGRN_CHUNK, width), _BF16),
            pltpu.VMEM((8 * ((n_chunks + 7) // 8), width), _F32),
            pltpu.VMEM((tm, width), _F32),
            pltpu.VMEM((POOL_HISTORY, width), _F32),
            pltpu.VMEM((n_heads, HEAD_DIM, HEAD_DIM), _F32),
        ],
        compiler_params=_params(),
        name=f"mixer_l{layer}",
    )(*inputs, *convert)


def _ffn(x, small, weights, fg, *, layer, final_norm, convert=(), convert_layer=0):
    batch, seq, d_model = x.shape
    tm = _token_tile(seq, FFN_TILE)
    n2g, cw, cb = small
    w_up, w_dn = weights
    d_ff = w_dn.shape[0]
    block = 256 if d_ff % 256 == 0 else 128
    grid = (batch, seq // tm)
    tile = pl.BlockSpec((None, tm, d_model), lambda b, i: (b, i, 0))
    per_layer = functools.partial(_layer_of, layer=layer)
    plans = [_cast_plan(w, convert_layer, *grid) for w in convert]
    inputs = (x, n2g, w_up, cw, cb, w_dn, fg)
    in_specs = [tile, per_layer(n2g), _resident(w_up.shape), per_layer(cw), per_layer(cb), _resident(w_dn.shape),
                _resident(fg.shape)]
    body = functools.partial(_ffn_kernel, block=block, final_norm=final_norm)
    return pl.pallas_call(
        _with_side_casts(body, len(inputs), len(convert)),
        out_shape=[jax.ShapeDtypeStruct(x.shape, _F32)] + [p[2] for p in plans],
        grid=grid,
        in_specs=in_specs + [p[0] for p in plans],
        out_specs=[tile] + [p[1] for p in plans],
        scratch_shapes=[
            pltpu.VMEM((tm, d_ff), _BF16),
            pltpu.VMEM((CONV_TAIL, 2 * d_ff), _F32),
        ],
        compiler_params=_params(),
        name=f"ffn_l{layer}",
    )(*inputs, *convert)


def kernel(x, norm1_g, w_in, b_gate, pool_w, pool_scale, lb_logits, hgrn_norm_g, w_pa, w_pb, w_o, norm2_g,
           w_up, conv_w, conv_b, w_down, final_g):
    depth = w_in.shape[0]
    x = x.astype(_F32)

    def row(a):
        return a.astype(_F32)[:, None, :]

    mixer_small = (row(norm1_g), row(b_gate), pool_w.astype(_BF16), row(pool_scale), row(hgrn_norm_g))
    ffn_small = (row(norm2_g), conv_w.astype(_F32), row(conv_b))
    mixer_f32 = tuple(w.astype(_F32) for w in (w_in, w_pa, w_pb, w_o))
    ffn_f32 = tuple(w.astype(_F32) for w in (w_up, w_down))
    lbl = lb_logits.astype(_F32)
    fg = final_g.astype(_F32)[None, :]
    mixer_w = tuple(w[0].astype(_BF16) for w in mixer_f32)
    for l in range(depth):
        x, *ffn_w = _mixer(x, mixer_small, mixer_w, lbl, layer=l, convert=ffn_f32)
        last = l == depth - 1
        x, *mixer_w = _ffn(x, ffn_small, ffn_w, fg, layer=l, final_norm=last,
                           convert=() if last else mixer_f32, convert_layer=l + 1)
    return x
```
